```python
import jax, jax.numpy as jnp
from jax import lax
import numpy as np

D_MODEL = 1024
BATCH = 2
SEQ = 16384
DEPTH = 1
DEC_BATCH = 2
DEC_SEQ = 8192
PAST_LEN = 128

HEAD_DIM = 64
N_HEADS_A = 8
N_KV_A = 2
N_HEADS_B = 8
N_KV_B = 2
WIDTH_A = N_HEADS_A * HEAD_DIM
WIDTH_B = N_HEADS_B * HEAD_DIM
MIX_WIDTH = WIDTH_A + WIDTH_B
KV_A = N_KV_A * HEAD_DIM
KV_B = N_KV_B * HEAD_DIM
IN_WIDTH = WIDTH_A + 2 * KV_A + WIDTH_B + 2 * KV_B
BLOCK = 128
WINDOW = 128
GRID_W = 64
ROPE_THETA = 10000.0
D_FF = 2816
CONV_WIDTH = 3
EPS = 1e-6

kernel_name = "hymba_axial_rope_window_sink_convffn_encoder"


def rmsnorm(x, g):
    xf = x.astype(jnp.float32)
    y = xf * lax.rsqrt(jnp.mean(xf * xf, axis=-1, keepdims=True) + EPS)
    return (y * g.astype(jnp.float32)).astype(x.dtype)


def axial_rope_tables(S):
    rows = S // GRID_W
    row = jnp.repeat(jnp.arange(rows, dtype=jnp.float32), GRID_W)
    col = jnp.tile(jnp.arange(GRID_W, dtype=jnp.float32), rows)
    axis_dim = HEAD_DIM // 2
    inv = ROPE_THETA ** (-jnp.arange(0, axis_dim, 2, dtype=jnp.float32) / axis_dim)
    ang = jnp.stack([row[:, None] * inv, col[:, None] * inv], axis=1)
    return jnp.cos(ang), jnp.sin(ang)


def apply_axial_rope(x, cos, sin):
    B, S, H, Dh = x.shape
    xr = x.astype(jnp.float32).reshape(B, S, H, 2, 2, Dh // 4)
    x1, x2 = xr[..., 0, :], xr[..., 1, :]
    c = cos[None, :, None]
    s = sin[None, :, None]
    out = jnp.stack([x1 * c - x2 * s, x2 * c + x1 * s], axis=-2)
    return out.reshape(B, S, H, Dh).astype(x.dtype)


def global_attention(q, k, v):
    B, S, HQ, Dh = q.shape
    HKV = k.shape[2]
    G = HQ // HKV
    nb = S // BLOCK
    qb = (q * (Dh ** -0.5)).reshape(B, nb, BLOCK, HKV, G, Dh).transpose(1, 0, 2, 3, 4, 5)

    def one_block(qi):
        s = jnp.einsum('bqkgd,bskd->bkgqs', qi, k).astype(jnp.float32)
        p = jax.nn.softmax(s, axis=-1)
        return jnp.einsum('bkgqs,bskd->bqkgd', p.astype(v.dtype), v)

    o = lax.map(one_block, qb)
    return o.transpose(1, 0, 2, 3, 4, 5).reshape(B, S, HQ * Dh)


def window_attention(q, k, v, sink, slopes):
    B, S, HQ, Dh = q.shape
    HKV = k.shape[2]
    G = HQ // HKV
    nb = S // BLOCK
    qb = (q * (Dh ** -0.5)).reshape(B, nb, BLOCK, HKV, G, Dh)
    pad = ((0, 0), (BLOCK, BLOCK), (0, 0), (0, 0))
    kp = jnp.pad(k, pad).reshape(B, nb + 2, BLOCK, HKV, Dh)
    vp = jnp.pad(v, pad).reshape(B, nb + 2, BLOCK, HKV, Dh)
    kb = jnp.concatenate([kp[:, :-2], kp[:, 1:-1], kp[:, 2:]], axis=2)
    vb = jnp.concatenate([vp[:, :-2], vp[:, 1:-1], vp[:, 2:]], axis=2)
    s = jnp.einsum('bnqkgd,bnskd->bnkgqs', qb, kb).astype(jnp.float32)
    blk = jnp.arange(nb)[:, None]
    tpos = blk * BLOCK + jnp.arange(BLOCK)[None, :]
    spos = (blk - 1) * BLOCK + jnp.arange(3 * BLOCK)[None, :]
    dist = jnp.abs(tpos[:, :, None] - spos[:, None, :])
    valid = (dist <= WINDOW) & ((spos >= 0) & (spos < S))[:, None, :]
    distf = dist.astype(jnp.float32)
    bias = -slopes.reshape(HKV, G)[None, :, :, None, None] * distf[:, None, None]
    s = jnp.where(valid[:, None, None], s + bias, -jnp.inf)
    sink_l = sink.astype(jnp.float32).reshape(HKV, G)[None, None, :, :, None, None]
    m = jnp.maximum(jnp.max(s, axis=-1, keepdims=True), sink_l)
    p = jnp.exp(s - m)
    p = p / (jnp.sum(p, axis=-1, keepdims=True) + jnp.exp(sink_l - m))
    o = jnp.einsum('bnkgqs,bnskd->bnqkgd', p.astype(v.dtype), vb)
    return o.reshape(B, S, HQ * Dh)


def depthwise_conv_centred(u, w, b):
    up = jnp.pad(u, ((0, 0), (1, 1), (0, 0)))
    return up[:, :-2] * w[0] + up[:, 1:-1] * w[1] + up[:, 2:] * w[2] + b


def encoder_layer(x, norm_mix_g, w_in, qnorm_a_g, knorm_a_g, qnorm_b_g, knorm_b_g, sink_b,
                  out_norm_a_g, out_norm_b_g, w_out, norm_ffn_g, w_up, conv_w, conv_b, w_down):
    B, S, _ = x.shape
    cos, sin = axial_rope_tables(S)
    slopes = jnp.exp2(-8.0 * jnp.arange(1, N_HEADS_B + 1, dtype=jnp.float32) / N_HEADS_B)

    h = rmsnorm(x, norm_mix_g)
    proj = h @ w_in
    o1 = WIDTH_A
    o2 = o1 + KV_A
    o3 = o2 + KV_A
    o4 = o3 + WIDTH_B
    o5 = o4 + KV_B
    qa = proj[..., :o1].reshape(B, S, N_HEADS_A, HEAD_DIM)
    ka = proj[..., o1:o2].reshape(B, S, N_KV_A, HEAD_DIM)
    va = proj[..., o2:o3].reshape(B, S, N_KV_A, HEAD_DIM)
    qb = proj[..., o3:o4].reshape(B, S, N_HEADS_B, HEAD_DIM)
    kb = proj[..., o4:o5].reshape(B, S, N_KV_B, HEAD_DIM)
    vb = proj[..., o5:].reshape(B, S, N_KV_B, HEAD_DIM)

    qa = apply_axial_rope(rmsnorm(qa, qnorm_a_g), cos, sin)
    ka = apply_axial_rope(rmsnorm(ka, knorm_a_g), cos, sin)
    ya = global_attention(qa, ka, va)

    qb = rmsnorm(qb, qnorm_b_g)
    kb = rmsnorm(kb, knorm_b_g)
    yb = window_attention(qb, kb, vb, sink_b, slopes)

    y = jnp.concatenate([rmsnorm(ya, out_norm_a_g), rmsnorm(yb, out_norm_b_g)], axis=-1)
    x = x + y @ w_out

    h2 = rmsnorm(x, norm_ffn_g)
    u = depthwise_conv_centred(h2 @ w_up, conv_w, conv_b)
    gate, up = u[..., :D_FF], u[..., D_FF:]
    x = x + (jax.nn.gelu(gate, approximate=True) * up) @ w_down
    return x


def setup_inputs(seed: int = 0) -> dict:
    key = jax.random.key(seed)
    ks = jax.random.split(key, 20)
    f32 = jnp.float32

    def nrm(k, shape, scale):
        return jax.random.normal(k, shape, f32) * scale

    def gain(k, shape):
        return 1.0 + 0.02 * jax.random.normal(k, shape, f32)

    return {
        "x_prompt": jax.random.normal(ks[0], (BATCH, SEQ, D_MODEL), f32),
        "x_sample": jax.random.normal(ks[1], (DEC_BATCH, DEC_SEQ, D_MODEL), f32),
        "norm_mix_g": gain(ks[2], (DEPTH, D_MODEL)),
        "w_in": nrm(ks[3], (DEPTH, D_MODEL, IN_WIDTH), D_MODEL ** -0.5),
        "qnorm_a_g": gain(ks[4], (DEPTH, HEAD_DIM)),
        "knorm_a_g": gain(ks[5], (DEPTH, HEAD_DIM)),
        "qnorm_b_g": gain(ks[6], (DEPTH, HEAD_DIM)),
        "knorm_b_g": gain(ks[7], (DEPTH, HEAD_DIM)),
        "sink_b": nrm(ks[8], (DEPTH, N_HEADS_B), 0.5),
        "out_norm_a_g": gain(ks[9], (DEPTH, WIDTH_A)),
        "out_norm_b_g": gain(ks[10], (DEPTH, WIDTH_B)),
        "w_out": nrm(ks[11], (DEPTH, MIX_WIDTH, D_MODEL), MIX_WIDTH ** -0.5),
        "norm_ffn_g": gain(ks[12], (DEPTH, D_MODEL)),
        "w_up": nrm(ks[13], (DEPTH, D_MODEL, 2 * D_FF), D_MODEL ** -0.5),
        "conv_w": nrm(ks[14], (DEPTH, CONV_WIDTH, 2 * D_FF), CONV_WIDTH ** -0.5),
        "conv_b": nrm(ks[15], (DEPTH, 2 * D_FF), 0.01),
        "w_down": nrm(ks[16], (DEPTH, D_FF, D_MODEL), D_FF ** -0.5),
    }


def reference(x_prompt, x_sample, norm_mix_g, w_in, qnorm_a_g, knorm_a_g, qnorm_b_g, knorm_b_g,
              sink_b, out_norm_a_g, out_norm_b_g, w_out, norm_ffn_g, w_up, conv_w, conv_b, w_down):
    y_prompt = x_prompt
    y_sample = x_sample
    for l in range(DEPTH):
        params = (norm_mix_g[l], w_in[l], qnorm_a_g[l], knorm_a_g[l], qnorm_b_g[l], knorm_b_g[l],
                  sink_b[l], out_norm_a_g[l], out_norm_b_g[l], w_out[l], norm_ffn_g[l],
                  w_up[l], conv_w[l], conv_b[l], w_down[l])
        y_prompt = encoder_layer(y_prompt, *params)
        y_sample = encoder_layer(y_sample, *params)
    return (y_prompt, y_sample)
```

```python
import functools
import math

import jax
import jax.numpy as jnp
from jax import lax
from jax.experimental import pallas as pl
from jax.experimental.pallas import tpu as pltpu

D_MODEL = 1024
HEAD_DIM = 64
N_HEADS = 8
N_KV = 2
GROUP = N_HEADS // N_KV
WIDTH = N_HEADS * HEAD_DIM
KV_WIDTH = N_KV * HEAD_DIM
IN_WIDTH = 2 * (WIDTH + 2 * KV_WIDTH)
WINDOW = 128
GRID_W = 64
ROPE_THETA = 10000.0
D_FF = 2816
EPS = 1e-6
LOG2E = math.log2(math.e)
Q_SCALE = HEAD_DIM ** -0.5 * LOG2E

LANES = 128
NORM_CHUNK = 256
HALO = 16
FF_CHUNK = 256
VMEM_LIMIT = 56 * 1024 * 1024

F32 = jnp.float32
BF16 = jnp.bfloat16


def _tile(n, want):
    t = min(n, want)
    assert n % t == 0, (n, t)
    return t


def _rms(x, g):
    return x * lax.rsqrt(jnp.mean(x * x, axis=-1, keepdims=True) + EPS) * g


def _proj_kernel(x_ref, g_ref, w_ref, hg_ref, cos_ref, sin_ref,
                 qa_ref, ka_ref, va_ref, qb_ref, kb_ref, vb_ref):
    h = _rms(x_ref[0], g_ref[...]).astype(BF16)
    proj = jnp.dot(h, w_ref[...], preferred_element_type=F32)

    r = lax.broadcasted_iota(jnp.int32, (NORM_CHUNK, NORM_CHUNK), 0) // HEAD_DIM
    c = lax.broadcasted_iota(jnp.int32, (NORM_CHUNK, NORM_CHUNK), 1) // HEAD_DIM
    seg = (r == c).astype(BF16)

    def head_norm(lo):
        v = proj[:, lo:lo + NORM_CHUNK]
        sq = v * v
        sq_hi = sq.astype(BF16)
        sq_lo = (sq - sq_hi.astype(F32)).astype(BF16)
        ss = (jnp.dot(sq_hi, seg, preferred_element_type=F32)
              + jnp.dot(sq_lo, seg, preferred_element_type=F32))
        return v * lax.rsqrt(ss * (1.0 / HEAD_DIM) + EPS) * hg_ref[:, lo:lo + NORM_CHUNK]

    lane = lax.broadcasted_iota(jnp.int32, (1, LANES), 1)
    first_half = (lane % 32) < 16
    cos = cos_ref[...]
    sin = sin_ref[...]

    def rope(v):
        partner = jnp.where(first_half, pltpu.roll(v, LANES - 16, 1), pltpu.roll(v, 16, 1))
        return v * cos + partner * sin

    def put_t(ref, first_head, v):
        vt = v.T
        ref[0, first_head] = vt[:HEAD_DIM].astype(ref.dtype)
        ref[0, first_head + 1] = vt[HEAD_DIM:].astype(ref.dtype)

    def put(ref, v):
        ref[0, 0] = v[:, :HEAD_DIM].astype(ref.dtype)
        ref[0, 1] = v[:, HEAD_DIM:].astype(ref.dtype)

    o_ka = WIDTH
    o_qb = WIDTH + 2 * KV_WIDTH
    o_kb = o_qb + WIDTH
    for j in range(WIDTH // NORM_CHUNK):
        qa = head_norm(j * NORM_CHUNK)
        qb = head_norm(o_qb + j * NORM_CHUNK)
        for p in range(NORM_CHUNK // LANES):
            hd = (j * NORM_CHUNK + p * LANES) // HEAD_DIM
            put_t(qa_ref, hd, rope(qa[:, p * LANES:(p + 1) * LANES]) * Q_SCALE)
            put_t(qb_ref, hd, qb[:, p * LANES:(p + 1) * LANES] * Q_SCALE)
    ka = head_norm(o_ka)
    put(ka_ref, rope(ka[:, :LANES]))
    put_t(va_ref, 0, proj[:, o_ka + KV_WIDTH:o_ka + 2 * KV_WIDTH])
    kb = head_norm(o_kb)
    put(kb_ref, kb[:, :LANES])
    put_t(vb_ref, 0, proj[:, o_kb + KV_WIDTH:o_kb + 2 * KV_WIDTH])


def _proj(x, g, w_in, head_gain, cos, sin):
    B, S, D = x.shape
    T = _tile(S, 512)
    qt_shape = jax.ShapeDtypeStruct((B, N_HEADS, HEAD_DIM, S), BF16)
    k_shape = jax.ShapeDtypeStruct((B, N_KV, S, HEAD_DIM), BF16)
    vt_shape = jax.ShapeDtypeStruct((B, N_KV, HEAD_DIM, S), BF16)
    qt_spec = pl.BlockSpec((1, N_HEADS, HEAD_DIM, T), lambda b, i: (b, 0, 0, i))
    k_spec = pl.BlockSpec((1, N_KV, T, HEAD_DIM), lambda b, i: (b, 0, i, 0))
    vt_spec = pl.BlockSpec((1, N_KV, HEAD_DIM, T), lambda b, i: (b, 0, 0, i))
    const = lambda b, i: (0, 0)
    return pl.pallas_call(
        _proj_kernel,
        grid=(B, S // T),
        in_specs=[
            pl.BlockSpec((1, T, D), lambda b, i: (b, i, 0)),
            pl.BlockSpec((1, D), const),
            pl.BlockSpec((D, IN_WIDTH), const),
            pl.BlockSpec((1, IN_WIDTH), const),
            pl.BlockSpec((T, LANES), lambda b, i: (i, 0)),
            pl.BlockSpec((T, LANES), lambda b, i: (i, 0)),
        ],
        out_specs=[qt_spec, k_spec, vt_spec, qt_spec, k_spec, vt_spec],
        out_shape=[qt_shape, k_shape, vt_shape, qt_shape, k_shape, vt_shape],
        compiler_params=pltpu.CompilerParams(
            dimension_semantics=("parallel", "parallel"), vmem_limit_bytes=VMEM_LIMIT),
        name="proj",
    )(x, g, w_in, head_gain, cos, sin)


def _attn_a_kernel(qt_ref, k_ref, vt_ref, o_ref, *, tk, nk):
    tq = qt_ref.shape[-1]
    qts = [qt_ref[0, g] for g in range(GROUP)]

    def body(i, carry):
        ks = pl.multiple_of(i * tk, tk)
        k = k_ref[0, 0, pl.ds(ks, tk), :]
        vt = vt_ref[0, 0, :, pl.ds(ks, tk)]
        new = []
        for g in range(GROUP):
            m, l, acc = carry[g]
            s = jnp.dot(k, qts[g], preferred_element_type=F32)
            m_new = jnp.maximum(m, jnp.max(s, axis=0, keepdims=True))
            p = jnp.exp2(s - m_new)
            alpha = jnp.exp2(m - m_new)
            l = alpha * l + jnp.sum(p, axis=0, keepdims=True)
            acc = alpha * acc + jnp.dot(vt, p.astype(BF16), preferred_element_type=F32)
            new.append((m_new, l, acc))
        return tuple(new)

    init = tuple((jnp.full((1, tq), -jnp.inf, F32), jnp.zeros((1, tq), F32),
                  jnp.zeros((HEAD_DIM, tq), F32)) for _ in range(GROUP))
    final = lax.fori_loop(0, nk, body, init)
    o = jnp.concatenate([acc / l for (_, l, acc) in final], axis=0)
    o_ref[0] = o.T


def _attn_a(qt, k, vt):
    B, _, _, S = qt.shape
    tq = _tile(S, 512)
    tk = _tile(S, 512)
    return pl.pallas_call(
        functools.partial(_attn_a_kernel, tk=tk, nk=S // tk),
        grid=(B, N_KV, S // tq),
        in_specs=[
            pl.BlockSpec((1, GROUP, HEAD_DIM, tq), lambda b, h, i: (b, h, 0, i)),
            pl.BlockSpec((1, 1, S, HEAD_DIM), lambda b, h, i: (b, h, 0, 0)),
            pl.BlockSpec((1, 1, HEAD_DIM, S), lambda b, h, i: (b, h, 0, 0)),
        ],
        out_specs=pl.BlockSpec((1, tq, GROUP * HEAD_DIM), lambda b, h, i: (b, i, h)),
        out_shape=jax.ShapeDtypeStruct((B, S, WIDTH), F32),
        compiler_params=pltpu.CompilerParams(
            dimension_semantics=("parallel", "parallel", "parallel"), vmem_limit_bytes=VMEM_LIMIT),
        name="attn_a",
    )(qt, k, vt)


def _attn_b_kernel(hp_ref, qt_ref, k_ref, vt_ref, o_ref, *, seq, kw):
    tq = qt_ref.shape[-1]
    h = pl.program_id(1)
    t0 = pl.program_id(2) * tq
    start = pl.multiple_of(jnp.clip(t0 - WINDOW, 0, seq - kw), LANES)
    k = k_ref[0, 0, pl.ds(start, kw), :]
    vt = vt_ref[0, 0, :, pl.ds(start, kw)]
    spos = start + lax.broadcasted_iota(jnp.int32, (kw, tq), 0)
    tpos = t0 + lax.broadcasted_iota(jnp.int32, (kw, tq), 1)
    dist = jnp.abs(tpos - spos)
    valid = dist <= WINDOW
    distf = dist.astype(F32)
    outs = []
    for g in range(GROUP):
        slope = hp_ref[0, h * GROUP + g]
        sink = hp_ref[1, h * GROUP + g]
        s = jnp.dot(k, qt_ref[0, g], preferred_element_type=F32)
        s = jnp.where(valid, s - slope * distf, -jnp.inf)
        m = jnp.maximum(jnp.max(s, axis=0, keepdims=True), sink)
        p = jnp.exp2(s - m)
        denom = jnp.sum(p, axis=0, keepdims=True) + jnp.exp2(sink - m)
        outs.append(jnp.dot(vt, p.astype(BF16), preferred_element_type=F32) / denom)
    o_ref[0] = jnp.concatenate(outs, axis=0).T


def _attn_b(head_params, qt, k, vt):
    B, _, _, S = qt.shape
    tq = _tile(S, 256)
    kw = min(S, tq + 2 * WINDOW)
    return pl.pallas_call(
        functools.partial(_attn_b_kernel, seq=S, kw=kw),
        grid=(B, N_KV, S // tq),
        in_specs=[
            pl.BlockSpec(memory_space=pltpu.SMEM),
            pl.BlockSpec((1, GROUP, HEAD_DIM, tq), lambda b, h, i: (b, h, 0, i)),
            pl.BlockSpec((1, 1, S, HEAD_DIM), lambda b, h, i: (b, h, 0, 0)),
            pl.BlockSpec((1, 1, HEAD_DIM, S), lambda b, h, i: (b, h, 0, 0)),
        ],
        out_specs=pl.BlockSpec((1, tq, GROUP * HEAD_DIM), lambda b, h, i: (b, i, h)),
        out_shape=jax.ShapeDtypeStruct((B, S, WIDTH), F32),
        compiler_params=pltpu.CompilerParams(
            dimension_semantics=("parallel", "parallel", "parallel"), vmem_limit_bytes=VMEM_LIMIT),
        name="attn_b",
    )(head_params, qt, k, vt)


def _outproj_kernel(ya_ref, yb_ref, x_ref, ga_ref, gb_ref, w_ref, o_ref):
    na = _rms(ya_ref[0], ga_ref[...]).astype(BF16)
    nb = _rms(yb_ref[0], gb_ref[...]).astype(BF16)
    o_ref[0] = (x_ref[0]
                + jnp.dot(na, w_ref[:WIDTH], preferred_element_type=F32)
                + jnp.dot(nb, w_ref[WIDTH:], preferred_element_type=F32))


def _outproj(ya, yb, x, ga, gb, w_out):
    B, S, D = x.shape
    T = _tile(S, 512)
    const = lambda b, i: (0, 0)
    row = lambda b, i: (b, i, 0)
    return pl.pallas_call(
        _outproj_kernel,
        grid=(B, S // T),
        in_specs=[
            pl.BlockSpec((1, T, WIDTH), row),
            pl.BlockSpec((1, T, WIDTH), row),
            pl.BlockSpec((1, T, D), row),
            pl.BlockSpec((1, WIDTH), const),
            pl.BlockSpec((1, WIDTH), const),
            pl.BlockSpec((2 * WIDTH, D), const),
        ],
        out_specs=pl.BlockSpec((1, T, D), row),
        out_shape=jax.ShapeDtypeStruct((B, S, D), F32),
        compiler_params=pltpu.CompilerParams(
            dimension_semantics=("parallel", "parallel"), vmem_limit_bytes=VMEM_LIMIT),
        name="outproj",
    )(ya, yb, x, ga, gb, w_out)


def _gelu_tanh(x):
    return 0.5 * x * (1.0 + jnp.tanh(math.sqrt(2.0 / math.pi) * (x + 0.044715 * (x * x * x))))


def _ffn_kernel(x_ref, prev_ref, next_ref, g_ref, wup_ref, cw_ref, cb_ref, wdn_ref, o_ref, h_ref):
    i = pl.program_id(1)
    T = x_ref.shape[1]
    g = g_ref[...]
    x = x_ref[0]
    h_prev = _rms(prev_ref[0], g) * (i > 0).astype(F32)
    h_next = _rms(next_ref[0], g) * (i < pl.num_programs(1) - 1).astype(F32)
    h_ref[:HALO] = h_prev.astype(BF16)
    h_ref[HALO:HALO + T] = _rms(x, g).astype(BF16)
    h_ref[HALO + T:] = h_next.astype(BF16)
    h = h_ref[...]
    rows = T + 2 * HALO

    def conv(u, lo):
        w = cw_ref[:, lo:lo + FF_CHUNK]
        up = pltpu.roll(u, 1, 0)[HALO:HALO + T]
        dn = pltpu.roll(u, rows - 1, 0)[HALO:HALO + T]
        return (up * w[0:1] + u[HALO:HALO + T] * w[1:2] + dn * w[2:3]
                + cb_ref[:, lo:lo + FF_CHUNK])

    acc = x
    for j in range(D_FF // FF_CHUNK):
        lo = j * FF_CHUNK
        gate = conv(jnp.dot(h, wup_ref[:, lo:lo + FF_CHUNK], preferred_element_type=F32), lo)
        up = conv(jnp.dot(h, wup_ref[:, D_FF + lo:D_FF + lo + FF_CHUNK],
                          preferred_element_type=F32), D_FF + lo)
        act = (_gelu_tanh(gate) * up).astype(BF16)
        acc = acc + jnp.dot(act, wdn_ref[lo:lo + FF_CHUNK], preferred_element_type=F32)
    o_ref[0] = acc


def _ffn(x, g, w_up, conv_w, conv_b, w_down):
    B, S, D = x.shape
    T = _tile(S, 512)
    hb = T // HALO
    n_hb = S // HALO
    const = lambda b, i: (0, 0)
    return pl.pallas_call(
        _ffn_kernel,
        grid=(B, S // T),
        in_specs=[
            pl.BlockSpec((1, T, D), lambda b, i: (b, i, 0)),
            pl.BlockSpec((1, HALO, D), lambda b, i: (b, jnp.maximum(i * hb - 1, 0), 0)),
            pl.BlockSpec((1, HALO, D), lambda b, i: (b, jnp.minimum((i + 1) * hb, n_hb - 1), 0)),
            pl.BlockSpec((1, D), const),
            pl.BlockSpec((D, 2 * D_FF), const),
            pl.BlockSpec((3, 2 * D_FF), const),
            pl.BlockSpec((1, 2 * D_FF), const),
            pl.BlockSpec((D_FF, D), const),
        ],
        out_specs=pl.BlockSpec((1, T, D), lambda b, i: (b, i, 0)),
        out_shape=jax.ShapeDtypeStruct((B, S, D), F32),
        scratch_shapes=[pltpu.VMEM((T + 2 * HALO, D), BF16)],
        compiler_params=pltpu.CompilerParams(
            dimension_semantics=("parallel", "parallel"), vmem_limit_bytes=VMEM_LIMIT),
        name="ffn",
    )(x, x, x, g, w_up, conv_w, conv_b, w_down)


def _rope_tables(S):
    t = jnp.arange(S, dtype=jnp.int32)
    pos = jnp.stack([(t // GRID_W).astype(F32), (t % GRID_W).astype(F32)], axis=1)
    axis_dim = HEAD_DIM // 2
    inv = ROPE_THETA ** (-jnp.arange(0, axis_dim, 2, dtype=F32) / axis_dim)
    ang = pos[:, :, None] * inv
    cos = jnp.cos(ang)
    sin = jnp.sin(ang)
    cos_h = jnp.concatenate([cos, cos], axis=-1).reshape(S, HEAD_DIM)
    sin_h = jnp.concatenate([-sin, sin], axis=-1).reshape(S, HEAD_DIM)
    reps = LANES // HEAD_DIM
    return jnp.tile(cos_h, (1, reps)), jnp.tile(sin_h, (1, reps))


def _layer(x, p):
    S = x.shape[1]
    cos, sin = _rope_tables(S)
    qa, ka, va, qb, kb, vb = _proj(x, p["norm_mix_g"], p["w_in"], p["head_gain"], cos, sin)
    ya = _attn_a(qa, ka, va)
    yb = _attn_b(p["head_params"], qb, kb, vb)
    x1 = _outproj(ya, yb, x, p["out_norm_a_g"], p["out_norm_b_g"], p["w_out"])
    return _ffn(x1, p["norm_ffn_g"], p["w_up"], p["conv_w"], p["conv_b"], p["w_down"])


def kernel(x_prompt, x_sample, norm_mix_g, w_in, qnorm_a_g, knorm_a_g, qnorm_b_g, knorm_b_g,
           sink_b, out_norm_a_g, out_norm_b_g, w_out, norm_ffn_g, w_up, conv_w, conv_b, w_down):
    depth = w_in.shape[0]
    y_prompt, y_sample = x_prompt, x_sample
    slopes = jnp.exp2(-8.0 * jnp.arange(1, N_HEADS + 1, dtype=F32) / N_HEADS)
    ones_kv = jnp.ones((KV_WIDTH,), F32)
    for l in range(depth):
        head_gain = jnp.concatenate([
            jnp.tile(qnorm_a_g[l], N_HEADS), jnp.tile(knorm_a_g[l], N_KV), ones_kv,
            jnp.tile(qnorm_b_g[l], N_HEADS), jnp.tile(knorm_b_g[l], N_KV), ones_kv])[None]
        p = {
            "norm_mix_g": norm_mix_g[l][None],
            "w_in": w_in[l].astype(BF16),
            "head_gain": head_gain,
            "head_params": jnp.stack([slopes, sink_b[l].astype(F32)]) * LOG2E,
            "out_norm_a_g": out_norm_a_g[l][None],
            "out_norm_b_g": out_norm_b_g[l][None],
            "w_out": w_out[l].astype(BF16),
            "norm_ffn_g": norm_ffn_g[l][None],
            "w_up": w_up[l].astype(BF16),
            "conv_w": conv_w[l],
            "conv_b": conv_b[l][None],
            "w_down": w_down[l].astype(BF16),
        }
        y_prompt = _layer(y_prompt, p)
        y_sample = _layer(y_sample, p)
    return (y_prompt, y_sample)
```

```python
import functools
import math

import jax
import jax.numpy as jnp
from jax import lax
from jax.experimental import pallas as pl
from jax.experimental.pallas import tpu as pltpu

D_MODEL = 1024
HEAD_DIM = 64
N_HEADS = 8
N_KV = 2
GROUP = N_HEADS // N_KV
WIDTH = N_HEADS * HEAD_DIM
KV_WIDTH = N_KV * HEAD_DIM
IN_WIDTH = 2 * (WIDTH + 2 * KV_WIDTH)
WINDOW = 128
GRID_W = 64
ROPE_THETA = 10000.0
D_FF = 2816
EPS = 1e-6
LOG2E = math.log2(math.e)
Q_SCALE = HEAD_DIM ** -0.5 * LOG2E

LANES = 128
NORM_CHUNK = 256
BF16_ROWS = 16
HALO = BF16_ROWS
SUM_ROWS = BF16_ROWS
AHEAD = 2
CHUNKS_PER_TRIP = 4
FF_CHUNK = 256
VMEM_LIMIT = 56 * 1024 * 1024

F32 = jnp.float32
BF16 = jnp.bfloat16


def _tile(n, want):
    t = min(n, want)
    assert n % t == 0, (n, t)
    return t


def _rms(x, g):
    return x * lax.rsqrt(jnp.mean(x * x, axis=-1, keepdims=True) + EPS) * g


def _proj_kernel(x_ref, g_ref, w_ref, hg_ref, cos_ref, sin_ref,
                 qa_ref, ka_ref, va_ref, qb_ref, kb_ref, vb_ref):
    h = _rms(x_ref[0], g_ref[...]).astype(BF16)
    proj = jnp.dot(h, w_ref[...], preferred_element_type=F32)

    r = lax.broadcasted_iota(jnp.int32, (NORM_CHUNK, NORM_CHUNK), 0) // HEAD_DIM
    c = lax.broadcasted_iota(jnp.int32, (NORM_CHUNK, NORM_CHUNK), 1) // HEAD_DIM
    seg = (r == c).astype(BF16)

    def head_norm(lo):
        v = proj[:, lo:lo + NORM_CHUNK]
        sq = v * v
        sq_hi = sq.astype(BF16)
        sq_lo = (sq - sq_hi.astype(F32)).astype(BF16)
        ss = (jnp.dot(sq_hi, seg, preferred_element_type=F32)
              + jnp.dot(sq_lo, seg, preferred_element_type=F32))
        return v * lax.rsqrt(ss * (1.0 / HEAD_DIM) + EPS) * hg_ref[:, lo:lo + NORM_CHUNK]

    lane = lax.broadcasted_iota(jnp.int32, (1, LANES), 1)
    first_half = (lane % 32) < 16
    cos = cos_ref[...]
    sin = sin_ref[...]

    def rope(v):
        partner = jnp.where(first_half, pltpu.roll(v, LANES - 16, 1), pltpu.roll(v, 16, 1))
        return v * cos + partner * sin

    def put_t(ref, first_head, v):
        vt = v.T
        ref[0, first_head] = vt[:HEAD_DIM].astype(ref.dtype)
        ref[0, first_head + 1] = vt[HEAD_DIM:].astype(ref.dtype)

    def put(ref, v):
        ref[0, 0] = v[:, :HEAD_DIM].astype(ref.dtype)
        ref[0, 1] = v[:, HEAD_DIM:].astype(ref.dtype)

    o_ka = WIDTH
    o_qb = WIDTH + 2 * KV_WIDTH
    o_kb = o_qb + WIDTH
    for j in range(WIDTH // NORM_CHUNK):
        qa = head_norm(j * NORM_CHUNK)
        qb = head_norm(o_qb + j * NORM_CHUNK)
        for p in range(NORM_CHUNK // LANES):
            hd = (j * NORM_CHUNK + p * LANES) // HEAD_DIM
            put_t(qa_ref, hd, rope(qa[:, p * LANES:(p + 1) * LANES]) * Q_SCALE)
            put_t(qb_ref, hd, qb[:, p * LANES:(p + 1) * LANES] * Q_SCALE)
    ka = head_norm(o_ka)
    put(ka_ref, rope(ka[:, :LANES]))
    put_t(va_ref, 0, proj[:, o_ka + KV_WIDTH:o_ka + 2 * KV_WIDTH])
    kb = head_norm(o_kb)
    put(kb_ref, kb[:, :LANES])
    put_t(vb_ref, 0, proj[:, o_kb + KV_WIDTH:o_kb + 2 * KV_WIDTH])


def _proj(x, g, w_in, head_gain, cos, sin):
    B, S, D = x.shape
    T = _tile(S, 512)
    qt_shape = jax.ShapeDtypeStruct((B, N_HEADS, HEAD_DIM, S), BF16)
    k_shape = jax.ShapeDtypeStruct((B, N_KV, S, HEAD_DIM), BF16)
    vt_shape = jax.ShapeDtypeStruct((B, N_KV, HEAD_DIM, S), BF16)
    qt_spec = pl.BlockSpec((1, N_HEADS, HEAD_DIM, T), lambda b, i: (b, 0, 0, i))
    k_spec = pl.BlockSpec((1, N_KV, T, HEAD_DIM), lambda b, i: (b, 0, i, 0))
    vt_spec = pl.BlockSpec((1, N_KV, HEAD_DIM, T), lambda b, i: (b, 0, 0, i))
    const = lambda b, i: (0, 0)
    return pl.pallas_call(
        _proj_kernel,
        grid=(B, S // T),
        in_specs=[
            pl.BlockSpec((1, T, D), lambda b, i: (b, i, 0)),
            pl.BlockSpec((1, D), const),
            pl.BlockSpec((D, IN_WIDTH), const),
            pl.BlockSpec((1, IN_WIDTH), const),
            pl.BlockSpec((T, LANES), lambda b, i: (i, 0)),
            pl.BlockSpec((T, LANES), lambda b, i: (i, 0)),
        ],
        out_specs=[qt_spec, k_spec, vt_spec, qt_spec, k_spec, vt_spec],
        out_shape=[qt_shape, k_shape, vt_shape, qt_shape, k_shape, vt_shape],
        compiler_params=pltpu.CompilerParams(
            dimension_semantics=("parallel", "parallel"), vmem_limit_bytes=VMEM_LIMIT),
        name="proj",
    )(x, g, w_in, head_gain, cos, sin)


def _attn_a_kernel(qt_ref, k_ref, vt_ref, o_ref, s_ref, *, tk, nk, cpb):
    tq = qt_ref.shape[-1]
    ones = jnp.ones((SUM_ROWS, tk), BF16)

    def scores(i, g):
        ks = i * tk if isinstance(i, int) else pl.multiple_of(i * tk, tk)
        return jnp.dot(k_ref[0, 0, pl.ds(ks, tk), :], qt_ref[0, g], preferred_element_type=F32)

    def item(w):
        return w // GROUP, w % GROUP

    for w in range(AHEAD):
        s_ref[w] = scores(*item(w))

    def body(j, carry):
        c0 = j * cpb
        state = list(carry)
        tiles = [s_ref[w] for w in range(AHEAD)]
        for u in range(cpb * GROUP):
            dc, g_next = item(u + AHEAD)
            tiles.append(scores(jnp.minimum(c0 + dc, nk - 1), g_next))
            dc, g = item(u)
            ks = pl.multiple_of((c0 + dc) * tk, tk)
            vt = jnp.concatenate([vt_ref[0, 0, :, pl.ds(ks, tk)], ones], axis=0)
            s = tiles[u]
            m, acc = state[g]
            m_new = jnp.maximum(m, jnp.max(s, axis=0, keepdims=True))
            p = jnp.exp2(s - m_new).astype(BF16)
            acc = jnp.exp2(m - m_new) * acc + jnp.dot(vt, p, preferred_element_type=F32)
            state[g] = (m_new, acc)
        for w in range(AHEAD):
            s_ref[w] = tiles[cpb * GROUP + w]
        return tuple(state)

    init = tuple((jnp.full((1, tq), -jnp.inf, F32), jnp.zeros((HEAD_DIM + SUM_ROWS, tq), F32))
                 for _ in range(GROUP))
    final = lax.fori_loop(0, nk // cpb, body, init)
    o = jnp.concatenate([acc[:HEAD_DIM] / acc[HEAD_DIM:HEAD_DIM + 1] for (_, acc) in final], axis=0)
    o_ref[0] = o.T


def _attn_a(qt, k, vt):
    B, _, _, S = qt.shape
    tq = _tile(S, 512)
    tk = _tile(S, 512)
    nk = S // tk
    cpb = math.gcd(nk, CHUNKS_PER_TRIP)
    return pl.pallas_call(
        functools.partial(_attn_a_kernel, tk=tk, nk=nk, cpb=cpb),
        grid=(B, N_KV, S // tq),
        in_specs=[
            pl.BlockSpec((1, GROUP, HEAD_DIM, tq), lambda b, h, i: (b, h, 0, i)),
            pl.BlockSpec((1, 1, S, HEAD_DIM), lambda b, h, i: (b, h, 0, 0)),
            pl.BlockSpec((1, 1, HEAD_DIM, S), lambda b, h, i: (b, h, 0, 0)),
        ],
        out_specs=pl.BlockSpec((1, tq, GROUP * HEAD_DIM), lambda b, h, i: (b, i, h)),
        out_shape=jax.ShapeDtypeStruct((B, S, WIDTH), F32),
        scratch_shapes=[pltpu.VMEM((AHEAD, tk, tq), F32)],
        compiler_params=pltpu.CompilerParams(
            dimension_semantics=("parallel", "parallel", "parallel"), vmem_limit_bytes=VMEM_LIMIT),
        name="attn_a",
    )(qt, k, vt)


def _attn_b_kernel(hp_ref, qt_ref, k_ref, vt_ref, o_ref, *, seq, kw):
    tq = qt_ref.shape[-1]
    h = pl.program_id(1)
    t0 = pl.program_id(2) * tq
    start = pl.multiple_of(jnp.clip(t0 - WINDOW, 0, seq - kw), LANES)
    k = k_ref[0, 0, pl.ds(start, kw), :]
    vt = vt_ref[0, 0, :, pl.ds(start, kw)]
    spos = start + lax.broadcasted_iota(jnp.int32, (kw, tq), 0)
    tpos = t0 + lax.broadcasted_iota(jnp.int32, (kw, tq), 1)
    dist = jnp.abs(tpos - spos)
    valid = dist <= WINDOW
    distf = dist.astype(F32)
    scores = [jnp.dot(k, qt_ref[0, g], preferred_element_type=F32) for g in range(GROUP)]
    outs = []
    for g in range(GROUP):
        slope = hp_ref[0, h * GROUP + g]
        sink = hp_ref[1, h * GROUP + g]
        s = jnp.where(valid, scores[g] - slope * distf, -jnp.inf)
        m = jnp.maximum(jnp.max(s, axis=0, keepdims=True), sink)
        p = jnp.exp2(s - m)
        denom = jnp.sum(p, axis=0, keepdims=True) + jnp.exp2(sink - m)
        outs.append(jnp.dot(vt, p.astype(BF16), preferred_element_type=F32) / denom)
    o_ref[0] = jnp.concatenate(outs, axis=0).T


def _attn_b(head_params, qt, k, vt):
    B, _, _, S = qt.shape
    tq = _tile(S, 256)
    kw = min(S, tq + 2 * WINDOW)
    return pl.pallas_call(
        functools.partial(_attn_b_kernel, seq=S, kw=kw),
        grid=(B, N_KV, S // tq),
        in_specs=[
            pl.BlockSpec(memory_space=pltpu.SMEM),
            pl.BlockSpec((1, GROUP, HEAD_DIM, tq), lambda b, h, i: (b, h, 0, i)),
            pl.BlockSpec((1, 1, S, HEAD_DIM), lambda b, h, i: (b, h, 0, 0)),
            pl.BlockSpec((1, 1, HEAD_DIM, S), lambda b, h, i: (b, h, 0, 0)),
        ],
        out_specs=pl.BlockSpec((1, tq, GROUP * HEAD_DIM), lambda b, h, i: (b, i, h)),
        out_shape=jax.ShapeDtypeStruct((B, S, WIDTH), F32),
        compiler_params=pltpu.CompilerParams(
            dimension_semantics=("parallel", "parallel", "parallel"), vmem_limit_bytes=VMEM_LIMIT),
        name="attn_b",
    )(head_params, qt, k, vt)


def _outproj_kernel(ya_ref, yb_ref, x_ref, ga_ref, gb_ref, w_ref, o_ref):
    na = _rms(ya_ref[0], ga_ref[...]).astype(BF16)
    nb = _rms(yb_ref[0], gb_ref[...]).astype(BF16)
    o_ref[0] = (x_ref[0]
                + jnp.dot(na, w_ref[:WIDTH], preferred_element_type=F32)
                + jnp.dot(nb, w_ref[WIDTH:], preferred_element_type=F32))


def _outproj(ya, yb, x, ga, gb, w_out):
    B, S, D = x.shape
    T = _tile(S, 512)
    const = lambda b, i: (0, 0)
    row = lambda b, i: (b, i, 0)
    return pl.pallas_call(
        _outproj_kernel,
        grid=(B, S // T),
        in_specs=[
            pl.BlockSpec((1, T, WIDTH), row),
            pl.BlockSpec((1, T, WIDTH), row),
            pl.BlockSpec((1, T, D), row),
            pl.BlockSpec((1, WIDTH), const),
            pl.BlockSpec((1, WIDTH), const),
            pl.BlockSpec((2 * WIDTH, D), const),
        ],
        out_specs=pl.BlockSpec((1, T, D), row),
        out_shape=jax.ShapeDtypeStruct((B, S, D), F32),
        compiler_params=pltpu.CompilerParams(
            dimension_semantics=("parallel", "parallel"), vmem_limit_bytes=VMEM_LIMIT),
        name="outproj",
    )(ya, yb, x, ga, gb, w_out)


def _gelu_tanh(x):
    return 0.5 * x * (1.0 + jnp.tanh(math.sqrt(2.0 / math.pi) * (x + 0.044715 * (x * x * x))))


def _ffn_kernel(x_ref, prev_ref, next_ref, g_ref, wup_ref, cw_ref, cb_ref, wdn_ref, o_ref, h_ref):
    i = pl.program_id(1)
    T = x_ref.shape[1]
    g = g_ref[...]
    x = x_ref[0]
    h_prev = _rms(prev_ref[0], g) * (i > 0).astype(F32)
    h_next = _rms(next_ref[0], g) * (i < pl.num_programs(1) - 1).astype(F32)
    h_ref[:HALO] = h_prev.astype(BF16)
    h_ref[HALO:HALO + T] = _rms(x, g).astype(BF16)
    h_ref[HALO + T:] = h_next.astype(BF16)
    h = h_ref[...]
    rows = T + 2 * HALO

    def conv(u, lo):
        w = cw_ref[:, lo:lo + FF_CHUNK]
        up = pltpu.roll(u, 1, 0)[HALO:HALO + T]
        dn = pltpu.roll(u, rows - 1, 0)[HALO:HALO + T]
        return (up * w[0:1] + u[HALO:HALO + T] * w[1:2] + dn * w[2:3]
                + cb_ref[:, lo:lo + FF_CHUNK])

    def up_proj(j):
        lo = j * FF_CHUNK
        return (jnp.dot(h, wup_ref[:, lo:lo + FF_CHUNK], preferred_element_type=F32),
                jnp.dot(h, wup_ref[:, D_FF + lo:D_FF + lo + FF_CHUNK], preferred_element_type=F32))

    n_chunks = D_FF // FF_CHUNK
    pending = [up_proj(j) for j in range(AHEAD)]
    acc = x
    for j in range(n_chunks):
        if j + AHEAD < n_chunks:
            pending.append(up_proj(j + AHEAD))
        lo = j * FF_CHUNK
        gate = conv(pending[j][0], lo)
        up = conv(pending[j][1], D_FF + lo)
        act = (_gelu_tanh(gate) * up).astype(BF16)
        acc = acc + jnp.dot(act, wdn_ref[lo:lo + FF_CHUNK], preferred_element_type=F32)
    o_ref[0] = acc


def _ffn(x, g, w_up, conv_w, conv_b, w_down):
    B, S, D = x.shape
    T = _tile(S, 512)
    hb = T // HALO
    n_hb = S // HALO
    const = lambda b, i: (0, 0)
    return pl.pallas_call(
        _ffn_kernel,
        grid=(B, S // T),
        in_specs=[
            pl.BlockSpec((1, T, D), lambda b, i: (b, i, 0)),
            pl.BlockSpec((1, HALO, D), lambda b, i: (b, jnp.maximum(i * hb - 1, 0), 0)),
            pl.BlockSpec((1, HALO, D), lambda b, i: (b, jnp.minimum((i + 1) * hb, n_hb - 1), 0)),
            pl.BlockSpec((1, D), const),
            pl.BlockSpec((D, 2 * D_FF), const),
            pl.BlockSpec((3, 2 * D_FF), const),
            pl.BlockSpec((1, 2 * D_FF), const),
            pl.BlockSpec((D_FF, D), const),
        ],
        out_specs=pl.BlockSpec((1, T, D), lambda b, i: (b, i, 0)),
        out_shape=jax.ShapeDtypeStruct((B, S, D), F32),
        scratch_shapes=[pltpu.VMEM((T + 2 * HALO, D), BF16)],
        compiler_params=pltpu.CompilerParams(
            dimension_semantics=("parallel", "parallel"), vmem_limit_bytes=VMEM_LIMIT),
        name="ffn",
    )(x, x, x, g, w_up, conv_w, conv_b, w_down)


def _rope_tables(S):
    t = jnp.arange(S, dtype=jnp.int32)
    pos = jnp.stack([(t // GRID_W).astype(F32), (t % GRID_W).astype(F32)], axis=1)
    axis_dim = HEAD_DIM // 2
    inv = ROPE_THETA ** (-jnp.arange(0, axis_dim, 2, dtype=F32) / axis_dim)
    ang = pos[:, :, None] * inv
    cos = jnp.cos(ang)
    sin = jnp.sin(ang)
    cos_h = jnp.concatenate([cos, cos], axis=-1).reshape(S, HEAD_DIM)
    sin_h = jnp.concatenate([-sin, sin], axis=-1).reshape(S, HEAD_DIM)
    reps = LANES // HEAD_DIM
    return jnp.tile(cos_h, (1, reps)), jnp.tile(sin_h, (1, reps))


def _layer(x, p):
    S = x.shape[1]
    cos, sin = _rope_tables(S)
    qa, ka, va, qb, kb, vb = _proj(x, p["norm_mix_g"], p["w_in"], p["head_gain"], cos, sin)
    ya = _attn_a(qa, ka, va)
    yb = _attn_b(p["head_params"], qb, kb, vb)
    x1 = _outproj(ya, yb, x, p["out_norm_a_g"], p["out_norm_b_g"], p["w_out"])
    return _ffn(x1, p["norm_ffn_g"], p["w_up"], p["conv_w"], p["conv_b"], p["w_down"])


def kernel(x_prompt, x_sample, norm_mix_g, w_in, qnorm_a_g, knorm_a_g, qnorm_b_g, knorm_b_g,
           sink_b, out_norm_a_g, out_norm_b_g, w_out, norm_ffn_g, w_up, conv_w, conv_b, w_down):
    depth = w_in.shape[0]
    y_prompt, y_sample = x_prompt, x_sample
    slopes = jnp.exp2(-8.0 * jnp.arange(1, N_HEADS + 1, dtype=F32) / N_HEADS)
    ones_kv = jnp.ones((KV_WIDTH,), F32)
    for l in range(depth):
        head_gain = jnp.concatenate([
            jnp.tile(qnorm_a_g[l], N_HEADS), jnp.tile(knorm_a_g[l], N_KV), ones_kv,
            jnp.tile(qnorm_b_g[l], N_HEADS), jnp.tile(knorm_b_g[l], N_KV), ones_kv])[None]
        p = {
            "norm_mix_g": norm_mix_g[l][None],
            "w_in": w_in[l].astype(BF16),
            "head_gain": head_gain,
            "head_params": jnp.stack([slopes, sink_b[l].astype(F32)]) * LOG2E,
            "out_norm_a_g": out_norm_a_g[l][None],
            "out_norm_b_g": out_norm_b_g[l][None],
            "w_out": w_out[l].astype(BF16),
            "norm_ffn_g": norm_ffn_g[l][None],
            "w_up": w_up[l].astype(BF16),
            "conv_w": conv_w[l],
            "conv_b": conv_b[l][None],
            "w_down": w_down[l].astype(BF16),
        }
        y_prompt = _layer(y_prompt, p)
        y_sample = _layer(y_sample, p)
    return (y_prompt, y_sample)
```

```python
import functools
import math

import jax
import jax.numpy as jnp
from jax import lax
from jax.experimental import pallas as pl
from jax.experimental.pallas import tpu as pltpu

D_MODEL = 1024
HEAD_DIM = 64
N_HEADS = 8
N_KV = 2
GROUP = N_HEADS // N_KV
WIDTH = N_HEADS * HEAD_DIM
KV_WIDTH = N_KV * HEAD_DIM
IN_WIDTH = 2 * (WIDTH + 2 * KV_WIDTH)
WINDOW = 128
GRID_W = 64
ROPE_THETA = 10000.0
D_FF = 2816
EPS = 1e-6
LOG2E = math.log2(math.e)
Q_SCALE = HEAD_DIM ** -0.5 * LOG2E

LANES = 128
NORM_CHUNK = 256
BF16_ROWS = 16
HALO = BF16_ROWS
SUM_ROWS = BF16_ROWS
AHEAD = 2
CHUNKS_PER_TRIP = 4
FF_CHUNK = 256
U_SLOTS = 3
EXP2_SAFE_RANGE = 50.0
VMEM_LIMIT = 56 * 1024 * 1024

F32 = jnp.float32
BF16 = jnp.bfloat16


def _tile(n, want):
    t = min(n, want)
    assert n % t == 0, (n, t)
    return t


def _rms(x, g):
    return x * lax.rsqrt(jnp.mean(x * x, axis=-1, keepdims=True) + EPS) * g


def _proj_kernel(x_ref, g_ref, w_ref, hg_ref, cos_ref, sin_ref,
                 qa_ref, ka_ref, va_ref, qb_ref, kb_ref, vb_ref):
    h = _rms(x_ref[0], g_ref[...]).astype(BF16)
    proj = jnp.dot(h, w_ref[...], preferred_element_type=F32)

    r = lax.broadcasted_iota(jnp.int32, (NORM_CHUNK, NORM_CHUNK), 0) // HEAD_DIM
    c = lax.broadcasted_iota(jnp.int32, (NORM_CHUNK, NORM_CHUNK), 1) // HEAD_DIM
    seg = (r == c).astype(BF16)

    def head_norm(lo):
        v = proj[:, lo:lo + NORM_CHUNK]
        sq = v * v
        sq_hi = sq.astype(BF16)
        sq_lo = (sq - sq_hi.astype(F32)).astype(BF16)
        ss = (jnp.dot(sq_hi, seg, preferred_element_type=F32)
              + jnp.dot(sq_lo, seg, preferred_element_type=F32))
        return v * lax.rsqrt(ss * (1.0 / HEAD_DIM) + EPS) * hg_ref[:, lo:lo + NORM_CHUNK]

    lane = lax.broadcasted_iota(jnp.int32, (1, LANES), 1)
    first_half = (lane % 32) < 16
    cos = cos_ref[...]
    sin = sin_ref[...]

    def rope(v):
        partner = jnp.where(first_half, pltpu.roll(v, LANES - 16, 1), pltpu.roll(v, 16, 1))
        return v * cos + partner * sin

    def put_t(ref, first_head, v):
        vt = v.T
        ref[0, first_head] = vt[:HEAD_DIM].astype(ref.dtype)
        ref[0, first_head + 1] = vt[HEAD_DIM:].astype(ref.dtype)

    def put(ref, v):
        ref[0, 0] = v[:, :HEAD_DIM].astype(ref.dtype)
        ref[0, 1] = v[:, HEAD_DIM:].astype(ref.dtype)

    o_ka = WIDTH
    o_qb = WIDTH + 2 * KV_WIDTH
    o_kb = o_qb + WIDTH
    for j in range(WIDTH // NORM_CHUNK):
        qa = head_norm(j * NORM_CHUNK)
        qb = head_norm(o_qb + j * NORM_CHUNK)
        for p in range(NORM_CHUNK // LANES):
            hd = (j * NORM_CHUNK + p * LANES) // HEAD_DIM
            put_t(qa_ref, hd, rope(qa[:, p * LANES:(p + 1) * LANES]) * Q_SCALE)
            put_t(qb_ref, hd, qb[:, p * LANES:(p + 1) * LANES] * Q_SCALE)
    ka = head_norm(o_ka)
    put(ka_ref, rope(ka[:, :LANES]))
    put_t(va_ref, 0, proj[:, o_ka + KV_WIDTH:o_ka + 2 * KV_WIDTH])
    kb = head_norm(o_kb)
    put(kb_ref, kb[:, :LANES])
    put_t(vb_ref, 0, proj[:, o_kb + KV_WIDTH:o_kb + 2 * KV_WIDTH])


def _proj(x, g, w_in, head_gain, cos, sin):
    B, S, D = x.shape
    T = _tile(S, 512)
    qt_shape = jax.ShapeDtypeStruct((B, N_HEADS, HEAD_DIM, S), BF16)
    k_shape = jax.ShapeDtypeStruct((B, N_KV, S, HEAD_DIM), BF16)
    vt_shape = jax.ShapeDtypeStruct((B, N_KV, HEAD_DIM, S), BF16)
    qt_spec = pl.BlockSpec((1, N_HEADS, HEAD_DIM, T), lambda b, i: (b, 0, 0, i))
    k_spec = pl.BlockSpec((1, N_KV, T, HEAD_DIM), lambda b, i: (b, 0, i, 0))
    vt_spec = pl.BlockSpec((1, N_KV, HEAD_DIM, T), lambda b, i: (b, 0, 0, i))
    const = lambda b, i: (0, 0)
    return pl.pallas_call(
        _proj_kernel,
        grid=(B, S // T),
        in_specs=[
            pl.BlockSpec((1, T, D), lambda b, i: (b, i, 0)),
            pl.BlockSpec((1, D), const),
            pl.BlockSpec((D, IN_WIDTH), const),
            pl.BlockSpec((1, IN_WIDTH), const),
            pl.BlockSpec((T, LANES), lambda b, i: (i, 0)),
            pl.BlockSpec((T, LANES), lambda b, i: (i, 0)),
        ],
        out_specs=[qt_spec, k_spec, vt_spec, qt_spec, k_spec, vt_spec],
        out_shape=[qt_shape, k_shape, vt_shape, qt_shape, k_shape, vt_shape],
        compiler_params=pltpu.CompilerParams(
            dimension_semantics=("parallel", "parallel"), vmem_limit_bytes=VMEM_LIMIT),
        name="proj",
    )(x, g, w_in, head_gain, cos, sin)


def _attn_a_kernel(bound_ref, qt_ref, k_ref, vt_ref, o_ref, s_ref, *, tk, nk, cpb):
    tq = qt_ref.shape[-1]
    ones = jnp.ones((SUM_ROWS, tk), BF16)

    def scores(i, g):
        ks = i * tk if isinstance(i, int) else pl.multiple_of(i * tk, tk)
        return jnp.dot(k_ref[0, 0, pl.ds(ks, tk), :], qt_ref[0, g], preferred_element_type=F32)

    def item(w):
        return w // GROUP, w % GROUP

    def run(online):
        for w in range(AHEAD):
            s_ref[w] = scores(*item(w))

        def body(j, carry):
            c0 = j * cpb
            state = list(carry)
            tiles = [s_ref[w] for w in range(AHEAD)]
            for u in range(cpb * GROUP):
                dc, g_next = item(u + AHEAD)
                tiles.append(scores(jnp.minimum(c0 + dc, nk - 1), g_next))
                dc, g = item(u)
                ks = pl.multiple_of((c0 + dc) * tk, tk)
                vt = jnp.concatenate([vt_ref[0, 0, :, pl.ds(ks, tk)], ones], axis=0)
                s = tiles[u]
                if online:
                    m, acc = state[g]
                    m_new = jnp.maximum(m, jnp.max(s, axis=0, keepdims=True))
                    p = jnp.exp2(s - m_new).astype(BF16)
                    acc = jnp.exp2(m - m_new) * acc + jnp.dot(vt, p, preferred_element_type=F32)
                    state[g] = (m_new, acc)
                else:
                    p = jnp.exp2(s)
                    l, acc = state[g]
                    state[g] = (l + jnp.sum(p, axis=0, keepdims=True),
                                acc + jnp.dot(vt[:HEAD_DIM], p.astype(BF16), preferred_element_type=F32))
            for w in range(AHEAD):
                s_ref[w] = tiles[cpb * GROUP + w]
            return tuple(state)

        acc0 = jnp.zeros((HEAD_DIM + SUM_ROWS, tq), F32)
        if online:
            init = tuple((jnp.full((1, tq), -jnp.inf, F32), acc0) for _ in range(GROUP))
            accs = [acc for (_, acc) in lax.fori_loop(0, nk // cpb, body, init)]
        else:
            init = tuple((jnp.zeros((1, tq), F32), jnp.zeros((HEAD_DIM, tq), F32)) for _ in range(GROUP))
            accs = [jnp.concatenate([acc, l], axis=0) for (l, acc) in lax.fori_loop(0, nk // cpb, body, init)]
        o = jnp.concatenate([acc[:HEAD_DIM] / acc[HEAD_DIM:HEAD_DIM + 1] for acc in accs], axis=0)
        o_ref[0] = o.T

    small = bound_ref[0] <= EXP2_SAFE_RANGE
    pl.when(small)(lambda: run(False))
    pl.when(jnp.logical_not(small))(lambda: run(True))


def _attn_a(score_bound, qt, k, vt):
    B, _, _, S = qt.shape
    tq = _tile(S, 512)
    tk = _tile(S, 512)
    nk = S // tk
    cpb = math.gcd(nk, CHUNKS_PER_TRIP)
    return pl.pallas_call(
        functools.partial(_attn_a_kernel, tk=tk, nk=nk, cpb=cpb),
        grid=(B, N_KV, S // tq),
        in_specs=[
            pl.BlockSpec(memory_space=pltpu.SMEM),
            pl.BlockSpec((1, GROUP, HEAD_DIM, tq), lambda b, h, i: (b, h, 0, i)),
            pl.BlockSpec((1, 1, S, HEAD_DIM), lambda b, h, i: (b, h, 0, 0)),
            pl.BlockSpec((1, 1, HEAD_DIM, S), lambda b, h, i: (b, h, 0, 0)),
        ],
        out_specs=pl.BlockSpec((1, tq, GROUP * HEAD_DIM), lambda b, h, i: (b, i, h)),
        out_shape=jax.ShapeDtypeStruct((B, S, WIDTH), F32),
        scratch_shapes=[pltpu.VMEM((AHEAD, tk, tq), F32)],
        compiler_params=pltpu.CompilerParams(
            dimension_semantics=("parallel", "parallel", "parallel"), vmem_limit_bytes=VMEM_LIMIT),
        name="attn_a",
    )(score_bound, qt, k, vt)


def _attn_b_kernel(hp_ref, qt_ref, k_ref, vt_ref, o_ref, *, seq, kw):
    tq = qt_ref.shape[-1]
    h = pl.program_id(1)
    t0 = pl.program_id(2) * tq
    start = pl.multiple_of(jnp.clip(t0 - WINDOW, 0, seq - kw), LANES)
    k = k_ref[0, 0, pl.ds(start, kw), :]
    vt = vt_ref[0, 0, :, pl.ds(start, kw)]
    spos = start + lax.broadcasted_iota(jnp.int32, (kw, tq), 0)
    tpos = t0 + lax.broadcasted_iota(jnp.int32, (kw, tq), 1)
    dist = jnp.abs(tpos - spos)
    valid = dist <= WINDOW
    distf = dist.astype(F32)
    scores = [jnp.dot(k, qt_ref[0, g], preferred_element_type=F32) for g in range(GROUP)]
    outs = []
    for g in range(GROUP):
        slope = hp_ref[0, h * GROUP + g]
        sink = hp_ref[1, h * GROUP + g]
        s = jnp.where(valid, scores[g] - slope * distf, -jnp.inf)
        m = jnp.maximum(jnp.max(s, axis=0, keepdims=True), sink)
        p = jnp.exp2(s - m)
        denom = jnp.sum(p, axis=0, keepdims=True) + jnp.exp2(sink - m)
        outs.append(jnp.dot(vt, p.astype(BF16), preferred_element_type=F32) / denom)
    o_ref[0] = jnp.concatenate(outs, axis=0).T


def _attn_b(head_params, qt, k, vt):
    B, _, _, S = qt.shape
    tq = _tile(S, 256)
    kw = min(S, tq + 2 * WINDOW)
    return pl.pallas_call(
        functools.partial(_attn_b_kernel, seq=S, kw=kw),
        grid=(B, N_KV, S // tq),
        in_specs=[
            pl.BlockSpec(memory_space=pltpu.SMEM),
            pl.BlockSpec((1, GROUP, HEAD_DIM, tq), lambda b, h, i: (b, h, 0, i)),
            pl.BlockSpec((1, 1, S, HEAD_DIM), lambda b, h, i: (b, h, 0, 0)),
            pl.BlockSpec((1, 1, HEAD_DIM, S), lambda b, h, i: (b, h, 0, 0)),
        ],
        out_specs=pl.BlockSpec((1, tq, GROUP * HEAD_DIM), lambda b, h, i: (b, i, h)),
        out_shape=jax.ShapeDtypeStruct((B, S, WIDTH), F32),
        compiler_params=pltpu.CompilerParams(
            dimension_semantics=("parallel", "parallel", "parallel"), vmem_limit_bytes=VMEM_LIMIT),
        name="attn_b",
    )(head_params, qt, k, vt)


def _outproj_kernel(ya_ref, yb_ref, x_ref, ga_ref, gb_ref, w_ref, o_ref):
    na = _rms(ya_ref[0], ga_ref[...]).astype(BF16)
    nb = _rms(yb_ref[0], gb_ref[...]).astype(BF16)
    o_ref[0] = (x_ref[0]
                + jnp.dot(na, w_ref[:WIDTH], preferred_element_type=F32)
                + jnp.dot(nb, w_ref[WIDTH:], preferred_element_type=F32))


def _outproj(ya, yb, x, ga, gb, w_out):
    B, S, D = x.shape
    T = _tile(S, 512)
    const = lambda b, i: (0, 0)
    row = lambda b, i: (b, i, 0)
    return pl.pallas_call(
        _outproj_kernel,
        grid=(B, S // T),
        in_specs=[
            pl.BlockSpec((1, T, WIDTH), row),
            pl.BlockSpec((1, T, WIDTH), row),
            pl.BlockSpec((1, T, D), row),
            pl.BlockSpec((1, WIDTH), const),
            pl.BlockSpec((1, WIDTH), const),
            pl.BlockSpec((2 * WIDTH, D), const),
        ],
        out_specs=pl.BlockSpec((1, T, D), row),
        out_shape=jax.ShapeDtypeStruct((B, S, D), F32),
        compiler_params=pltpu.CompilerParams(
            dimension_semantics=("parallel", "parallel"), vmem_limit_bytes=VMEM_LIMIT),
        name="outproj",
    )(ya, yb, x, ga, gb, w_out)


def _gelu_tanh(x):
    c = math.sqrt(2.0 / math.pi)
    hx = 0.5 * x
    return hx + hx * jnp.tanh(x * (c + (c * 0.044715) * (x * x)))


def _ffn_kernel(x_ref, prev_ref, next_ref, g_ref, wup_ref, cw_ref, cb_ref, wdn_ref, o_ref,
                h_ref, u_ref, act_ref):
    i = pl.program_id(1)
    T = x_ref.shape[1]
    g = g_ref[...]
    h_prev = _rms(prev_ref[0], g) * (i > 0).astype(F32)
    h_next = _rms(next_ref[0], g) * (i < pl.num_programs(1) - 1).astype(F32)
    h_ref[:HALO] = h_prev.astype(BF16)
    h_ref[HALO:HALO + T] = _rms(x_ref[0], g).astype(BF16)
    h_ref[HALO + T:] = h_next.astype(BF16)
    h = h_ref[...]

    for j in range(D_FF // FF_CHUNK):
        slot = j % U_SLOTS
        cols = (j * FF_CHUNK, D_FF + j * FF_CHUNK)
        for part, lo in enumerate(cols):
            u_ref[slot, part] = jnp.dot(h, wup_ref[:, lo:lo + FF_CHUNK], preferred_element_type=F32)

        def conv(part, lo):
            w = cw_ref[:, lo:lo + FF_CHUNK]
            return (u_ref[slot, part, HALO - 1:HALO - 1 + T] * w[0:1]
                    + u_ref[slot, part, HALO:HALO + T] * w[1:2]
                    + u_ref[slot, part, HALO + 1:HALO + 1 + T] * w[2:3]
                    + cb_ref[:, lo:lo + FF_CHUNK])

        act = _gelu_tanh(conv(0, cols[0])) * conv(1, cols[1])
        act_ref[:, cols[0]:cols[0] + FF_CHUNK] = act.astype(BF16)
    o_ref[0] = x_ref[0] + jnp.dot(act_ref[...], wdn_ref[...], preferred_element_type=F32)


def _ffn(x, g, w_up, conv_w, conv_b, w_down):
    B, S, D = x.shape
    T = _tile(S, 512)
    hb = T // HALO
    n_hb = S // HALO
    const = lambda b, i: (0, 0)
    return pl.pallas_call(
        _ffn_kernel,
        grid=(B, S // T),
        in_specs=[
            pl.BlockSpec((1, T, D), lambda b, i: (b, i, 0)),
            pl.BlockSpec((1, HALO, D), lambda b, i: (b, jnp.maximum(i * hb - 1, 0), 0)),
            pl.BlockSpec((1, HALO, D), lambda b, i: (b, jnp.minimum((i + 1) * hb, n_hb - 1), 0)),
            pl.BlockSpec((1, D), const),
            pl.BlockSpec((D, 2 * D_FF), const),
            pl.BlockSpec((3, 2 * D_FF), const),
            pl.BlockSpec((1, 2 * D_FF), const),
            pl.BlockSpec((D_FF, D), const),
        ],
        out_specs=pl.BlockSpec((1, T, D), lambda b, i: (b, i, 0)),
        out_shape=jax.ShapeDtypeStruct((B, S, D), F32),
        scratch_shapes=[pltpu.VMEM((T + 2 * HALO, D), BF16),
                        pltpu.VMEM((U_SLOTS, 2, T + 2 * HALO, FF_CHUNK), F32),
                        pltpu.VMEM((T, D_FF), BF16)],
        compiler_params=pltpu.CompilerParams(
            dimension_semantics=("parallel", "parallel"), vmem_limit_bytes=VMEM_LIMIT),
        name="ffn",
    )(x, x, x, g, w_up, conv_w, conv_b, w_down)


def _rope_tables(S):
    t = jnp.arange(S, dtype=jnp.int32)
    pos = jnp.stack([(t // GRID_W).astype(F32), (t % GRID_W).astype(F32)], axis=1)
    axis_dim = HEAD_DIM // 2
    inv = ROPE_THETA ** (-jnp.arange(0, axis_dim, 2, dtype=F32) / axis_dim)
    ang = pos[:, :, None] * inv
    cos = jnp.cos(ang)
    sin = jnp.sin(ang)
    cos_h = jnp.concatenate([cos, cos], axis=-1).reshape(S, HEAD_DIM)
    sin_h = jnp.concatenate([-sin, sin], axis=-1).reshape(S, HEAD_DIM)
    reps = LANES // HEAD_DIM
    return jnp.tile(cos_h, (1, reps)), jnp.tile(sin_h, (1, reps))


def _score_bound(gq, gk):
    bound = HEAD_DIM * jnp.max(jnp.abs(gq)) * jnp.max(jnp.abs(gk)) * Q_SCALE * (1.0 + 2.0 ** -7) ** 2
    return bound.astype(F32).reshape(1)


def _layer(x, p):
    S = x.shape[1]
    cos, sin = _rope_tables(S)
    qa, ka, va, qb, kb, vb = _proj(x, p["norm_mix_g"], p["w_in"], p["head_gain"], cos, sin)
    ya = _attn_a(p["score_bound_a"], qa, ka, va)
    yb = _attn_b(p["head_params"], qb, kb, vb)
    x1 = _outproj(ya, yb, x, p["out_norm_a_g"], p["out_norm_b_g"], p["w_out"])
    return _ffn(x1, p["norm_ffn_g"], p["w_up"], p["conv_w"], p["conv_b"], p["w_down"])


def kernel(x_prompt, x_sample, norm_mix_g, w_in, qnorm_a_g, knorm_a_g, qnorm_b_g, knorm_b_g,
           sink_b, out_norm_a_g, out_norm_b_g, w_out, norm_ffn_g, w_up, conv_w, conv_b, w_down):
    depth = w_in.shape[0]
    y_prompt, y_sample = x_prompt, x_sample
    slopes = jnp.exp2(-8.0 * jnp.arange(1, N_HEADS + 1, dtype=F32) / N_HEADS)
    ones_kv = jnp.ones((KV_WIDTH,), F32)
    for l in range(depth):
        head_gain = jnp.concatenate([
            jnp.tile(qnorm_a_g[l], N_HEADS), jnp.tile(knorm_a_g[l], N_KV), ones_kv,
            jnp.tile(qnorm_b_g[l], N_HEADS), jnp.tile(knorm_b_g[l], N_KV), ones_kv])[None]
        p = {
            "norm_mix_g": norm_mix_g[l][None],
            "w_in": w_in[l].astype(BF16),
            "head_gain": head_gain,
            "score_bound_a": _score_bound(qnorm_a_g[l], knorm_a_g[l]),
            "head_params": jnp.stack([slopes, sink_b[l].astype(F32)]) * LOG2E,
            "out_norm_a_g": out_norm_a_g[l][None],
            "out_norm_b_g": out_norm_b_g[l][None],
            "w_out": w_out[l].astype(BF16),
            "norm_ffn_g": norm_ffn_g[l][None],
            "w_up": w_up[l].astype(BF16),
            "conv_w": conv_w[l],
            "conv_b": conv_b[l][None],
            "w_down": w_down[l].astype(BF16),
        }
        y_prompt = _layer(y_prompt, p)
        y_sample = _layer(y_sample, p)
    return (y_prompt, y_sample)
```

```python
import functools
import math

import jax
import jax.numpy as jnp
from jax import lax
from jax.experimental import pallas as pl
from jax.experimental.pallas import tpu as pltpu

D_MODEL = 1024
HEAD_DIM = 64
N_HEADS = 8
N_KV = 2
GROUP = N_HEADS // N_KV
WIDTH = N_HEADS * HEAD_DIM
KV_WIDTH = N_KV * HEAD_DIM
IN_WIDTH = 2 * (WIDTH + 2 * KV_WIDTH)
WINDOW = 128
GRID_W = 64
ROPE_THETA = 10000.0
D_FF = 2816
EPS = 1e-6
LOG2E = math.log2(math.e)
Q_SCALE = HEAD_DIM ** -0.5 * LOG2E

LANES = 128
NORM_CHUNK = 256
BF16_ROWS = 16
HALO = BF16_ROWS
SUM_ROWS = BF16_ROWS
AHEAD = 1
CHUNKS_PER_TRIP = 8
FF_CHUNK = 256
U_SLOTS = 3
EXP2_SAFE_RANGE = 50.0
B_TILES_PER_STEP = 4
B_SCORE_SAFE = 30.0
B_SINK_SAFE = 60.0
VMEM_LIMIT =56 * 1024 * 1024

F32 = jnp.float32
BF16 = jnp.bfloat16


def _tile(n, want):
    t = min(n, want)
    assert n % t == 0, (n, t)
    return t


def _rms(x, g):
    return x * lax.rsqrt(jnp.mean(x * x, axis=-1, keepdims=True) + EPS) * g


def _proj_kernel(x_ref, g_ref, w_ref, hg_ref, cos_ref, sin_ref,
                 qa_ref, ka_ref, va_ref, qb_ref, kb_ref, vb_ref):
    h = _rms(x_ref[0], g_ref[...]).astype(BF16)
    r = lax.broadcasted_iota(jnp.int32, (NORM_CHUNK, NORM_CHUNK), 0) // HEAD_DIM
    c = lax.broadcasted_iota(jnp.int32, (NORM_CHUNK, NORM_CHUNK), 1) // HEAD_DIM
    seg = (r == c).astype(BF16)

    def project(lo):
        return jnp.dot(h, w_ref[:, lo:lo + NORM_CHUNK], preferred_element_type=F32)

    def head_norm(v, lo):
        sq = v * v
        sq_hi = sq.astype(BF16)
        sq_lo = (sq - sq_hi.astype(F32)).astype(BF16)
        ss = (jnp.dot(sq_hi, seg, preferred_element_type=F32)
              + jnp.dot(sq_lo, seg, preferred_element_type=F32))
        return v * lax.rsqrt(ss * (1.0 / HEAD_DIM) + EPS) * hg_ref[:, lo:lo + NORM_CHUNK]

    lane = lax.broadcasted_iota(jnp.int32, (1, LANES), 1)
    first_half = (lane % 32) < 16
    cos = cos_ref[...]
    sin = sin_ref[...]

    def rope(v):
        partner = jnp.where(first_half, pltpu.roll(v, LANES - 16, 1), pltpu.roll(v, 16, 1))
        return v * cos + partner * sin

    def put_t(ref, first_head, v):
        vt = v.T
        ref[0, first_head] = vt[:HEAD_DIM].astype(ref.dtype)
        ref[0, first_head + 1] = vt[HEAD_DIM:].astype(ref.dtype)

    def put(ref, v):
        ref[0, 0] = v[:, :HEAD_DIM].astype(ref.dtype)
        ref[0, 1] = v[:, HEAD_DIM:].astype(ref.dtype)

    def finish_q(v, lo, ref, first_head, rotary):
        q = head_norm(v, lo)
        for p in range(NORM_CHUNK // LANES):
            part = q[:, p * LANES:(p + 1) * LANES]
            put_t(ref, first_head + p * (LANES // HEAD_DIM), (rope(part) if rotary else part) * Q_SCALE)

    def finish_kv(v, lo, k_ref, v_ref, rotary):
        k = head_norm(v, lo)[:, :LANES]
        put(k_ref, rope(k) if rotary else k)
        put_t(v_ref, 0, v[:, LANES:])

    heads_per_chunk = NORM_CHUNK // HEAD_DIM
    o_qb = WIDTH + 2 * KV_WIDTH
    stages = []
    for j in range(WIDTH // NORM_CHUNK):
        stages.append((j * NORM_CHUNK, functools.partial(
            finish_q, ref=qa_ref, first_head=j * heads_per_chunk, rotary=True)))
        stages.append((o_qb + j * NORM_CHUNK, functools.partial(
            finish_q, ref=qb_ref, first_head=j * heads_per_chunk, rotary=False)))
    stages.append((WIDTH, functools.partial(finish_kv, k_ref=ka_ref, v_ref=va_ref, rotary=True)))
    stages.append((o_qb + WIDTH, functools.partial(finish_kv, k_ref=kb_ref, v_ref=vb_ref, rotary=False)))

    depth = AHEAD + 1
    pending = [project(lo) for lo, _ in stages[:depth]]
    for n, (lo, finish) in enumerate(stages):
        if n + depth < len(stages):
            pending.append(project(stages[n + depth][0]))
        finish(pending[n], lo)


def _proj(x, g, w_in, head_gain, cos, sin):
    B, S, D = x.shape
    T = _tile(S, 512)
    qt_shape = jax.ShapeDtypeStruct((B, N_HEADS, HEAD_DIM, S), BF16)
    k_shape = jax.ShapeDtypeStruct((B, N_KV, S, HEAD_DIM), BF16)
    vt_shape = jax.ShapeDtypeStruct((B, N_KV, HEAD_DIM, S), BF16)
    qt_spec = pl.BlockSpec((1, N_HEADS, HEAD_DIM, T), lambda b, i: (b, 0, 0, i))
    k_spec = pl.BlockSpec((1, N_KV, T, HEAD_DIM), lambda b, i: (b, 0, i, 0))
    vt_spec = pl.BlockSpec((1, N_KV, HEAD_DIM, T), lambda b, i: (b, 0, 0, i))
    const = lambda b, i: (0, 0)
    return pl.pallas_call(
        _proj_kernel,
        grid=(B, S // T),
        in_specs=[
            pl.BlockSpec((1, T, D), lambda b, i: (b, i, 0)),
            pl.BlockSpec((1, D), const),
            pl.BlockSpec((D, IN_WIDTH), const),
            pl.BlockSpec((1, IN_WIDTH), const),
            pl.BlockSpec((T, LANES), lambda b, i: (i, 0)),
            pl.BlockSpec((T, LANES), lambda b, i: (i, 0)),
        ],
        out_specs=[qt_spec, k_spec, vt_spec, qt_spec, k_spec, vt_spec],
        out_shape=[qt_shape, k_shape, vt_shape, qt_shape, k_shape, vt_shape],
        compiler_params=pltpu.CompilerParams(
            dimension_semantics=("parallel", "parallel"), vmem_limit_bytes=VMEM_LIMIT),
        name="proj",
    )(x, g, w_in, head_gain, cos, sin)


def _attn_a_kernel(bound_ref, qt_ref, k_ref, vt_ref, o_ref, s_ref, *, tk, nk, cpb):
    tq = qt_ref.shape[-1]
    ones = jnp.ones((SUM_ROWS, tk), BF16)

    def scores(i, g):
        ks = i * tk if isinstance(i, int) else pl.multiple_of(i * tk, tk)
        return jnp.dot(k_ref[0, 0, pl.ds(ks, tk), :], qt_ref[0, g], preferred_element_type=F32)

    def item(w):
        return w // GROUP, w % GROUP

    def run(online):
        for w in range(AHEAD):
            s_ref[w] = scores(*item(w))

        def body(j, carry):
            c0 = j * cpb
            state = list(carry)
            tiles = [s_ref[w] for w in range(AHEAD)]
            for u in range(cpb * GROUP):
                dc, g_next = item(u + AHEAD)
                tiles.append(scores(jnp.minimum(c0 + dc, nk - 1), g_next))
                dc, g = item(u)
                ks = pl.multiple_of((c0 + dc) * tk, tk)
                vt = jnp.concatenate([vt_ref[0, 0, :, pl.ds(ks, tk)], ones], axis=0)
                s = tiles[u]
                if online:
                    m, acc = state[g]
                    m_new = jnp.maximum(m, jnp.max(s, axis=0, keepdims=True))
                    p = jnp.exp2(s - m_new).astype(BF16)
                    acc = jnp.exp2(m - m_new) * acc + jnp.dot(vt, p, preferred_element_type=F32)
                    state[g] = (m_new, acc)
                else:
                    p = jnp.exp2(s)
                    l, acc = state[g]
                    state[g] = (l + jnp.sum(p, axis=0, keepdims=True),
                                acc + jnp.dot(vt[:HEAD_DIM], p.astype(BF16), preferred_element_type=F32))
            for w in range(AHEAD):
                s_ref[w] = tiles[cpb * GROUP + w]
            return tuple(state)

        acc0 = jnp.zeros((HEAD_DIM + SUM_ROWS, tq), F32)
        if online:
            init = tuple((jnp.full((1, tq), -jnp.inf, F32), acc0) for _ in range(GROUP))
            accs = [acc for (_, acc) in lax.fori_loop(0, nk // cpb, body, init)]
        else:
            init = tuple((jnp.zeros((1, tq), F32), jnp.zeros((HEAD_DIM, tq), F32)) for _ in range(GROUP))
            accs = [jnp.concatenate([acc, l], axis=0) for (l, acc) in lax.fori_loop(0, nk // cpb, body, init)]
        o = jnp.concatenate([acc[:HEAD_DIM] / acc[HEAD_DIM:HEAD_DIM + 1] for acc in accs], axis=0)
        o_ref[0] = o.T

    small = bound_ref[0] <= EXP2_SAFE_RANGE
    pl.when(small)(lambda: run(False))
    pl.when(jnp.logical_not(small))(lambda: run(True))


def _attn_a(score_bound, qt, k, vt):
    B, _, _, S = qt.shape
    tq = _tile(S, 512)
    tk = _tile(S, 512)
    nk = S // tk
    cpb = math.gcd(nk, CHUNKS_PER_TRIP)
    return pl.pallas_call(
        functools.partial(_attn_a_kernel, tk=tk, nk=nk, cpb=cpb),
        grid=(B, N_KV, S // tq),
        in_specs=[
            pl.BlockSpec(memory_space=pltpu.SMEM),
            pl.BlockSpec((1, GROUP, HEAD_DIM, tq), lambda b, h, i: (b, h, 0, i)),
            pl.BlockSpec((1, 1, S, HEAD_DIM), lambda b, h, i: (b, h, 0, 0)),
            pl.BlockSpec((1, 1, HEAD_DIM, S), lambda b, h, i: (b, h, 0, 0)),
        ],
        out_specs=pl.BlockSpec((1, tq, GROUP * HEAD_DIM), lambda b, h, i: (b, i, h)),
        out_shape=jax.ShapeDtypeStruct((B, S, WIDTH), F32),
        scratch_shapes=[pltpu.VMEM((AHEAD, tk, tq), F32)],
        compiler_params=pltpu.CompilerParams(
            dimension_semantics=("parallel", "parallel", "parallel"), vmem_limit_bytes=VMEM_LIMIT),
        name="attn_a",
    )(score_bound, qt, k, vt)


def _attn_b_kernel(hp_ref, qt_ref, k_ref, vt_ref, o_ref, *, seq, kw, tq):
    h = pl.program_id(1)
    n_sub = qt_ref.shape[-1] // tq
    ones = jnp.ones((SUM_ROWS, kw), BF16)

    def tile(j, small):
        t0 = (pl.program_id(2) * n_sub + j) * tq
        start = pl.multiple_of(jnp.clip(t0 - WINDOW, 0, seq - kw), LANES)
        k = k_ref[0, 0, pl.ds(start, kw), :]
        vt = vt_ref[0, 0, :, pl.ds(start, kw)]
        spos = start + lax.broadcasted_iota(jnp.int32, (kw, tq), 0)
        tpos = t0 + lax.broadcasted_iota(jnp.int32, (kw, tq), 1)
        dist = jnp.abs(tpos - spos)
        valid = dist <= WINDOW
        distf = dist.astype(F32)
        scores = [jnp.dot(k, qt_ref[0, g, :, j * tq:(j + 1) * tq], preferred_element_type=F32)
                  for g in range(GROUP)]
        outs = []
        if small:
            far = jnp.where(valid, distf, jnp.inf)
            vt_ext = jnp.concatenate([vt, ones], axis=0)
        for g in range(GROUP):
            slope = hp_ref[0, h * GROUP + g]
            sink = hp_ref[1, h * GROUP + g]
            if small:
                p = jnp.exp2(scores[g] - slope * far).astype(BF16)
                acc = jnp.dot(vt_ext, p, preferred_element_type=F32)
                denom = acc[HEAD_DIM:HEAD_DIM + 1] + jnp.exp2(jnp.full((1, tq), sink, F32))
                outs.append(acc[:HEAD_DIM] / denom)
            else:
                s = jnp.where(valid, scores[g] - slope * distf, -jnp.inf)
                m = jnp.maximum(jnp.max(s, axis=0, keepdims=True), sink)
                p = jnp.exp2(s - m)
                denom = jnp.sum(p, axis=0, keepdims=True) + jnp.exp2(sink - m)
                outs.append(jnp.dot(vt, p.astype(BF16), preferred_element_type=F32) / denom)
        o_ref[0, j * tq:(j + 1) * tq, :] = jnp.concatenate(outs, axis=0).T

    def run(small):
        for j in range(n_sub):
            tile(j, small)

    small = jnp.logical_and(hp_ref[2, 0] <= B_SCORE_SAFE, hp_ref[2, 1] <= B_SINK_SAFE)
    pl.when(small)(lambda: run(True))
    pl.when(jnp.logical_not(small))(lambda: run(False))


def _attn_b(head_params, qt, k, vt):
    B, _, _, S = qt.shape
    tq = _tile(S, 256)
    kw = min(S, tq + 2 * WINDOW)
    tqb = _tile(S, B_TILES_PER_STEP * tq)
    return pl.pallas_call(
        functools.partial(_attn_b_kernel, seq=S, kw=kw, tq=tq),
        grid=(B, N_KV, S // tqb),
        in_specs=[
            pl.BlockSpec(memory_space=pltpu.SMEM),
            pl.BlockSpec((1, GROUP, HEAD_DIM, tqb), lambda b, h, i: (b, h, 0, i)),
            pl.BlockSpec((1, 1, S, HEAD_DIM), lambda b, h, i: (b, h, 0, 0)),
            pl.BlockSpec((1, 1, HEAD_DIM, S), lambda b, h, i: (b, h, 0, 0)),
        ],
        out_specs=pl.BlockSpec((1, tqb, GROUP * HEAD_DIM), lambda b, h, i: (b, i, h)),
        out_shape=jax.ShapeDtypeStruct((B, S, WIDTH), F32),
        compiler_params=pltpu.CompilerParams(
            dimension_semantics=("parallel", "parallel", "parallel"), vmem_limit_bytes=VMEM_LIMIT),
        name="attn_b",
    )(head_params, qt, k, vt)


def _outproj_kernel(ya_ref, yb_ref, x_ref, ga_ref, gb_ref, w_ref, o_ref):
    na = _rms(ya_ref[0], ga_ref[...]).astype(BF16)
    nb = _rms(yb_ref[0], gb_ref[...]).astype(BF16)
    o_ref[0] = (x_ref[0]
                + jnp.dot(na, w_ref[:WIDTH], preferred_element_type=F32)
                + jnp.dot(nb, w_ref[WIDTH:], preferred_element_type=F32))


def _outproj(ya, yb, x, ga, gb, w_out):
    B, S, D = x.shape
    T = _tile(S, 512)
    const = lambda b, i: (0, 0)
    row = lambda b, i: (b, i, 0)
    return pl.pallas_call(
        _outproj_kernel,
        grid=(B, S // T),
        in_specs=[
            pl.BlockSpec((1, T, WIDTH), row),
            pl.BlockSpec((1, T, WIDTH), row),
            pl.BlockSpec((1, T, D), row),
            pl.BlockSpec((1, WIDTH), const),
            pl.BlockSpec((1, WIDTH), const),
            pl.BlockSpec((2 * WIDTH, D), const),
        ],
        out_specs=pl.BlockSpec((1, T, D), row),
        out_shape=jax.ShapeDtypeStruct((B, S, D), F32),
        compiler_params=pltpu.CompilerParams(
            dimension_semantics=("parallel", "parallel"), vmem_limit_bytes=VMEM_LIMIT),
        name="outproj",
    )(ya, yb, x, ga, gb, w_out)


def _gelu_tanh(x):
    c = math.sqrt(2.0 / math.pi)
    hx = 0.5 * x
    return hx + hx * jnp.tanh(x * (c + (c * 0.044715) * (x * x)))


def _ffn_kernel(x_ref, prev_ref, next_ref, g_ref, wup_ref, cw_ref, cb_ref, wdn_ref, o_ref,
                h_ref, u_ref, act_ref):
    i = pl.program_id(1)
    T = x_ref.shape[1]
    g = g_ref[...]
    h_prev = _rms(prev_ref[0], g) * (i > 0).astype(F32)
    h_next = _rms(next_ref[0], g) * (i < pl.num_programs(1) - 1).astype(F32)
    h_ref[:HALO] = h_prev.astype(BF16)
    h_ref[HALO:HALO + T] = _rms(x_ref[0], g).astype(BF16)
    h_ref[HALO + T:] = h_next.astype(BF16)
    h = h_ref[...]

    for j in range(D_FF // FF_CHUNK):
        slot = j % U_SLOTS
        cols = (j * FF_CHUNK, D_FF + j * FF_CHUNK)
        for part, lo in enumerate(cols):
            u_ref[slot, part] = jnp.dot(h, wup_ref[:, lo:lo + FF_CHUNK], preferred_element_type=F32)

        def conv(part, lo):
            w = cw_ref[:, lo:lo + FF_CHUNK]
            return (u_ref[slot, part, HALO - 1:HALO - 1 + T] * w[0:1]
                    + u_ref[slot, part, HALO:HALO + T] * w[1:2]
                    + u_ref[slot, part, HALO + 1:HALO + 1 + T] * w[2:3]
                    + cb_ref[:, lo:lo + FF_CHUNK])

        act = _gelu_tanh(conv(0, cols[0])) * conv(1, cols[1])
        act_ref[:, cols[0]:cols[0] + FF_CHUNK] = act.astype(BF16)
    o_ref[0] = x_ref[0] + jnp.dot(act_ref[...], wdn_ref[...], preferred_element_type=F32)


def _ffn(x, g, w_up, conv_w, conv_b, w_down):
    B, S, D = x.shape
    T = _tile(S, 512)
    hb = T // HALO
    n_hb = S // HALO
    const = lambda b, i: (0, 0)
    return pl.pallas_call(
        _ffn_kernel,
        grid=(B, S // T),
        in_specs=[
            pl.BlockSpec((1, T, D), lambda b, i: (b, i, 0)),
            pl.BlockSpec((1, HALO, D), lambda b, i: (b, jnp.maximum(i * hb - 1, 0), 0)),
            pl.BlockSpec((1, HALO, D), lambda b, i: (b, jnp.minimum((i + 1) * hb, n_hb - 1), 0)),
            pl.BlockSpec((1, D), const),
            pl.BlockSpec((D, 2 * D_FF), const),
            pl.BlockSpec((3, 2 * D_FF), const),
            pl.BlockSpec((1, 2 * D_FF), const),
            pl.BlockSpec((D_FF, D), const),
        ],
        out_specs=pl.BlockSpec((1, T, D), lambda b, i: (b, i, 0)),
        out_shape=jax.ShapeDtypeStruct((B, S, D), F32),
        scratch_shapes=[pltpu.VMEM((T + 2 * HALO, D), BF16),
                        pltpu.VMEM((U_SLOTS, 2, T + 2 * HALO, FF_CHUNK), F32),
                        pltpu.VMEM((T, D_FF), BF16)],
        compiler_params=pltpu.CompilerParams(
            dimension_semantics=("parallel", "parallel"), vmem_limit_bytes=VMEM_LIMIT),
        name="ffn",
    )(x, x, x, g, w_up, conv_w, conv_b, w_down)


def _rope_tables(S):
    t = jnp.arange(S, dtype=jnp.int32)
    pos = jnp.stack([(t // GRID_W).astype(F32), (t % GRID_W).astype(F32)], axis=1)
    axis_dim = HEAD_DIM // 2
    inv = ROPE_THETA ** (-jnp.arange(0, axis_dim, 2, dtype=F32) / axis_dim)
    ang = pos[:, :, None] * inv
    cos = jnp.cos(ang)
    sin = jnp.sin(ang)
    cos_h = jnp.concatenate([cos, cos], axis=-1).reshape(S, HEAD_DIM)
    sin_h = jnp.concatenate([-sin, sin], axis=-1).reshape(S, HEAD_DIM)
    reps = LANES // HEAD_DIM
    return jnp.tile(cos_h, (1, reps)), jnp.tile(sin_h, (1, reps))


def _score_bound(gq, gk):
    bound = HEAD_DIM * jnp.max(jnp.abs(gq)) * jnp.max(jnp.abs(gk)) * Q_SCALE * (1.0 + 2.0 ** -7) ** 2
    return bound.astype(F32).reshape(1)


def _head_params_b(slopes, sink, gq, gk):
    sink2 = sink.astype(F32) * LOG2E
    limits = jnp.zeros((N_HEADS,), F32).at[0].set(_score_bound(gq, gk)[0]).at[1].set(jnp.max(jnp.abs(sink2)))
    return jnp.stack([slopes * LOG2E, sink2, limits])


def _layer(x, p):
    S = x.shape[1]
    cos, sin = _rope_tables(S)
    qa, ka, va, qb, kb, vb = _proj(x, p["norm_mix_g"], p["w_in"], p["head_gain"], cos, sin)
    ya = _attn_a(p["score_bound_a"], qa, ka, va)
    yb = _attn_b(p["head_params"], qb, kb, vb)
    x1 = _outproj(ya, yb, x, p["out_norm_a_g"], p["out_norm_b_g"], p["w_out"])
    return _ffn(x1, p["norm_ffn_g"], p["w_up"], p["conv_w"], p["conv_b"], p["w_down"])


def kernel(x_prompt, x_sample, norm_mix_g, w_in, qnorm_a_g, knorm_a_g, qnorm_b_g, knorm_b_g,
           sink_b, out_norm_a_g, out_norm_b_g, w_out, norm_ffn_g, w_up, conv_w, conv_b, w_down):
    depth = w_in.shape[0]
    y_prompt, y_sample = x_prompt, x_sample
    slopes = jnp.exp2(-8.0 * jnp.arange(1, N_HEADS + 1, dtype=F32) / N_HEADS)
    ones_kv = jnp.ones((KV_WIDTH,), F32)
    for l in range(depth):
        head_gain = jnp.concatenate([
            jnp.tile(qnorm_a_g[l], N_HEADS), jnp.tile(knorm_a_g[l], N_KV), ones_kv,
            jnp.tile(qnorm_b_g[l], N_HEADS), jnp.tile(knorm_b_g[l], N_KV), ones_kv])[None]
        p = {
            "norm_mix_g": norm_mix_g[l][None],
            "w_in": w_in[l].astype(BF16),
            "head_gain": head_gain,
            "score_bound_a": _score_bound(qnorm_a_g[l], knorm_a_g[l]),
            "head_params": _head_params_b(slopes, sink_b[l], qnorm_b_g[l], knorm_b_g[l]),
            "out_norm_a_g": out_norm_a_g[l][None],
            "out_norm_b_g": out_norm_b_g[l][None],
            "w_out": w_out[l].astype(BF16),
            "norm_ffn_g": norm_ffn_g[l][None],
            "w_up": w_up[l].astype(BF16),
            "conv_w": conv_w[l],
            "conv_b": conv_b[l][None],
            "w_down": w_down[l].astype(BF16),
        }
        y_prompt = _layer(y_prompt, p)
        y_sample = _layer(y_sample, p)
    return (y_prompt, y_sample)
```

```python
import functools
import math

import jax
import jax.numpy as jnp
from jax import lax
from jax.experimental import pallas as pl
from jax.experimental.pallas import tpu as pltpu

D_MODEL = 1024
HEAD_DIM = 64
N_HEADS = 8
N_KV = 2
GROUP = N_HEADS // N_KV
WIDTH = N_HEADS * HEAD_DIM
KV_WIDTH = N_KV * HEAD_DIM
IN_WIDTH = 2 * (WIDTH + 2 * KV_WIDTH)
WINDOW = 128
GRID_W = 64
ROPE_THETA = 10000.0
D_FF = 2816
EPS = 1e-6
LOG2E = math.log2(math.e)
Q_SCALE = HEAD_DIM ** -0.5 * LOG2E

LANES = 128
NORM_CHUNK = 256
BF16_ROWS = 16
HALO = BF16_ROWS
SUM_ROWS = BF16_ROWS
AHEAD = 1
CHUNKS_PER_TRIP = 8
FF_CHUNK = 256
U_SLOTS = 3
EXP2_SAFE_RANGE = 50.0
B_TILES_PER_STEP = 4
B_SCORE_SAFE = 30.0
B_SINK_SAFE = 60.0
VMEM_LIMIT =56 * 1024 * 1024

F32 = jnp.float32
BF16 = jnp.bfloat16
F8 = jnp.float8_e4m3fn
F8_MAX = 448.0
Q8_SHIFT = 6
K8_SHIFT = 4
STACK = 4 * HEAD_DIM


def _tile(n, want):
    t = min(n, want)
    assert n % t == 0, (n, t)
    return t


def _rms(x, g):
    return x * lax.rsqrt(jnp.mean(x * x, axis=-1, keepdims=True) + EPS) * g


def _proj_kernel(x_ref, g_ref, w_ref, hg_ref, cos_ref, sin_ref,
                 qa_ref, ka_ref, va_ref, qb_ref, kb_ref, vb_ref, qa8_ref, ka8_ref):
    h = _rms(x_ref[0], g_ref[...]).astype(BF16)
    r = lax.broadcasted_iota(jnp.int32, (NORM_CHUNK, NORM_CHUNK), 0) // HEAD_DIM
    c = lax.broadcasted_iota(jnp.int32, (NORM_CHUNK, NORM_CHUNK), 1) // HEAD_DIM
    seg = (r == c).astype(BF16)

    def project(lo):
        return jnp.dot(h, w_ref[:, lo:lo + NORM_CHUNK], preferred_element_type=F32)

    def head_norm(v, lo):
        sq = v * v
        sq_hi = sq.astype(BF16)
        sq_lo = (sq - sq_hi.astype(F32)).astype(BF16)
        ss = (jnp.dot(sq_hi, seg, preferred_element_type=F32)
              + jnp.dot(sq_lo, seg, preferred_element_type=F32))
        return v * lax.rsqrt(ss * (1.0 / HEAD_DIM) + EPS) * hg_ref[:, lo:lo + NORM_CHUNK]

    lane = lax.broadcasted_iota(jnp.int32, (1, LANES), 1)
    first_half = (lane % 32) < 16
    cos = cos_ref[...]
    sin = sin_ref[...]

    def rope(v):
        partner = jnp.where(first_half, pltpu.roll(v, LANES - 16, 1), pltpu.roll(v, 16, 1))
        return v * cos + partner * sin

    def put_t(ref, first_head, v):
        vt = v.T
        ref[0, first_head] = vt[:HEAD_DIM].astype(ref.dtype)
        ref[0, first_head + 1] = vt[HEAD_DIM:].astype(ref.dtype)
        return vt

    def split8(x):
        hi = x.astype(F8)
        return hi, (x - hi.astype(F32)).astype(F8)

    def put_q8(first_head, vt):
        scaled = vt * (2.0 ** Q8_SHIFT)
        for i in range(LANES // HEAD_DIM):
            hi, lo = split8(scaled[i * HEAD_DIM:(i + 1) * HEAD_DIM])
            qa8_ref[0, first_head + i] = jnp.concatenate([hi, lo, hi, lo], axis=0)

    def put_k8(v):
        scaled = v * (2.0 ** K8_SHIFT)
        swapped = pltpu.roll(scaled, HEAD_DIM, 1)
        low = lane < HEAD_DIM
        for i, dup in enumerate((jnp.where(low, scaled, swapped), jnp.where(low, swapped, scaled))):
            hi, lo = split8(dup)
            ka8_ref[0, i] = jnp.concatenate([hi, lo], axis=1)

    def put(ref, v):
        ref[0, 0] = v[:, :HEAD_DIM].astype(ref.dtype)
        ref[0, 1] = v[:, HEAD_DIM:].astype(ref.dtype)

    def finish_q(v, lo, ref, first_head, rotary):
        q = head_norm(v, lo)
        for p in range(NORM_CHUNK // LANES):
            part = q[:, p * LANES:(p + 1) * LANES]
            hd = first_head + p * (LANES // HEAD_DIM)
            vt = put_t(ref, hd, (rope(part) if rotary else part) * Q_SCALE)
            if rotary:
                put_q8(hd, vt)

    def finish_kv(v, lo, k_ref, v_ref, rotary):
        k = head_norm(v, lo)[:, :LANES]
        k = rope(k) if rotary else k
        put(k_ref, k)
        if rotary:
            put_k8(k)
        put_t(v_ref, 0, v[:, LANES:])

    heads_per_chunk = NORM_CHUNK // HEAD_DIM
    o_qb = WIDTH + 2 * KV_WIDTH
    stages = []
    for j in range(WIDTH // NORM_CHUNK):
        stages.append((j * NORM_CHUNK, functools.partial(
            finish_q, ref=qa_ref, first_head=j * heads_per_chunk, rotary=True)))
        stages.append((o_qb + j * NORM_CHUNK, functools.partial(
            finish_q, ref=qb_ref, first_head=j * heads_per_chunk, rotary=False)))
    stages.append((WIDTH, functools.partial(finish_kv, k_ref=ka_ref, v_ref=va_ref, rotary=True)))
    stages.append((o_qb + WIDTH, functools.partial(finish_kv, k_ref=kb_ref, v_ref=vb_ref, rotary=False)))

    depth = AHEAD + 1
    pending = [project(lo) for lo, _ in stages[:depth]]
    for n, (lo, finish) in enumerate(stages):
        if n + depth < len(stages):
            pending.append(project(stages[n + depth][0]))
        finish(pending[n], lo)


def _proj(x, g, w_in, head_gain, cos, sin):
    B, S, D = x.shape
    T = _tile(S, 512)
    qt_shape = jax.ShapeDtypeStruct((B, N_HEADS, HEAD_DIM, S), BF16)
    k_shape = jax.ShapeDtypeStruct((B, N_KV, S, HEAD_DIM), BF16)
    vt_shape = jax.ShapeDtypeStruct((B, N_KV, HEAD_DIM, S), BF16)
    q8_shape = jax.ShapeDtypeStruct((B, N_HEADS, STACK, S), F8)
    k8_shape = jax.ShapeDtypeStruct((B, N_KV, S, STACK), F8)
    q8_spec = pl.BlockSpec((1, N_HEADS, STACK, T), lambda b, i: (b, 0, 0, i))
    k8_spec = pl.BlockSpec((1, N_KV, T, STACK), lambda b, i: (b, 0, i, 0))
    qt_spec = pl.BlockSpec((1, N_HEADS, HEAD_DIM, T), lambda b, i: (b, 0, 0, i))
    k_spec = pl.BlockSpec((1, N_KV, T, HEAD_DIM), lambda b, i: (b, 0, i, 0))
    vt_spec = pl.BlockSpec((1, N_KV, HEAD_DIM, T), lambda b, i: (b, 0, 0, i))
    const = lambda b, i: (0, 0)
    return pl.pallas_call(
        _proj_kernel,
        grid=(B, S // T),
        in_specs=[
            pl.BlockSpec((1, T, D), lambda b, i: (b, i, 0)),
            pl.BlockSpec((1, D), const),
            pl.BlockSpec((D, IN_WIDTH), const),
            pl.BlockSpec((1, IN_WIDTH), const),
            pl.BlockSpec((T, LANES), lambda b, i: (i, 0)),
            pl.BlockSpec((T, LANES), lambda b, i: (i, 0)),
        ],
        out_specs=[qt_spec, k_spec, vt_spec, qt_spec, k_spec, vt_spec, q8_spec, k8_spec],
        out_shape=[qt_shape, k_shape, vt_shape, qt_shape, k_shape, vt_shape, q8_shape, k8_shape],
        compiler_params=pltpu.CompilerParams(
            dimension_semantics=("parallel", "parallel"), vmem_limit_bytes=VMEM_LIMIT),
        name="proj",
    )(x, g, w_in, head_gain, cos, sin)


def _attn_a_kernel(bound_ref, qt_ref, k_ref, vt_ref, qt8_ref, k8_ref, o_ref, s_ref, *, tk, nk, cpb):
    tq = qt_ref.shape[-1]
    ones = jnp.ones((SUM_ROWS, tk), BF16)

    def scores(i, g, stacked8):
        ks = i * tk if isinstance(i, int) else pl.multiple_of(i * tk, tk)
        if stacked8:
            s = jnp.dot(k8_ref[0, 0, pl.ds(ks, tk), :], qt8_ref[0, g], preferred_element_type=F32)
            return s * (2.0 ** -(Q8_SHIFT + K8_SHIFT))
        return jnp.dot(k_ref[0, 0, pl.ds(ks, tk), :], qt_ref[0, g], preferred_element_type=F32)

    def item(w):
        return w // GROUP, w % GROUP

    def run(online, stacked8):
        for w in range(AHEAD):
            s_ref[w] = scores(*item(w), stacked8)

        def body(j, carry):
            c0 = j * cpb
            state = list(carry)
            tiles = [s_ref[w] for w in range(AHEAD)]
            for u in range(cpb * GROUP):
                dc, g_next = item(u + AHEAD)
                tiles.append(scores(jnp.minimum(c0 + dc, nk - 1), g_next, stacked8))
                dc, g = item(u)
                ks = pl.multiple_of((c0 + dc) * tk, tk)
                vt = jnp.concatenate([vt_ref[0, 0, :, pl.ds(ks, tk)], ones], axis=0)
                s = tiles[u]
                if online:
                    m, acc = state[g]
                    m_new = jnp.maximum(m, jnp.max(s, axis=0, keepdims=True))
                    p = jnp.exp2(s - m_new).astype(BF16)
                    acc = jnp.exp2(m - m_new) * acc + jnp.dot(vt, p, preferred_element_type=F32)
                    state[g] = (m_new, acc)
                else:
                    p = jnp.exp2(s)
                    l, acc = state[g]
                    state[g] = (l + jnp.sum(p, axis=0, keepdims=True),
                                acc + jnp.dot(vt[:HEAD_DIM], p.astype(BF16), preferred_element_type=F32))
            for w in range(AHEAD):
                s_ref[w] = tiles[cpb * GROUP + w]
            return tuple(state)

        acc0 = jnp.zeros((HEAD_DIM + SUM_ROWS, tq), F32)
        if online:
            init = tuple((jnp.full((1, tq), -jnp.inf, F32), acc0) for _ in range(GROUP))
            accs = [acc for (_, acc) in lax.fori_loop(0, nk // cpb, body, init)]
        else:
            init = tuple((jnp.zeros((1, tq), F32), jnp.zeros((HEAD_DIM, tq), F32)) for _ in range(GROUP))
            accs = [jnp.concatenate([acc, l], axis=0) for (l, acc) in lax.fori_loop(0, nk // cpb, body, init)]
        o = jnp.concatenate([acc[:HEAD_DIM] / acc[HEAD_DIM:HEAD_DIM + 1] for acc in accs], axis=0)
        o_ref[0] = o.T

    small = bound_ref[0] <= EXP2_SAFE_RANGE
    use8 = jnp.logical_and(small, bound_ref[1] > 0.5)
    pl.when(use8)(lambda: run(False, True))
    pl.when(jnp.logical_and(small, jnp.logical_not(use8)))(lambda: run(False, False))
    pl.when(jnp.logical_not(small))(lambda: run(True, False))


def _attn_a(score_bound, qt, k, vt, qt8, k8):
    B, _, _, S = qt.shape
    tq = _tile(S, 512)
    tk = _tile(S, 512)
    nk = S // tk
    cpb = math.gcd(nk, CHUNKS_PER_TRIP)
    return pl.pallas_call(
        functools.partial(_attn_a_kernel, tk=tk, nk=nk, cpb=cpb),
        grid=(B, N_KV, S // tq),
        in_specs=[
            pl.BlockSpec(memory_space=pltpu.SMEM),
            pl.BlockSpec((1, GROUP, HEAD_DIM, tq), lambda b, h, i: (b, h, 0, i)),
            pl.BlockSpec((1, 1, S, HEAD_DIM), lambda b, h, i: (b, h, 0, 0)),
            pl.BlockSpec((1, 1, HEAD_DIM, S), lambda b, h, i: (b, h, 0, 0)),
            pl.BlockSpec((1, GROUP, STACK, tq), lambda b, h, i: (b, h, 0, i)),
            pl.BlockSpec((1, 1, S, STACK), lambda b, h, i: (b, h, 0, 0)),
        ],
        out_specs=pl.BlockSpec((1, tq, GROUP * HEAD_DIM), lambda b, h, i: (b, i, h)),
        out_shape=jax.ShapeDtypeStruct((B, S, WIDTH), F32),
        scratch_shapes=[pltpu.VMEM((AHEAD, tk, tq), F32)],
        compiler_params=pltpu.CompilerParams(
            dimension_semantics=("parallel", "parallel", "parallel"), vmem_limit_bytes=VMEM_LIMIT),
        name="attn_a",
    )(score_bound, qt, k, vt, qt8, k8)


def _attn_b_kernel(hp_ref, qt_ref, k_ref, vt_ref, o_ref, *, seq, kw, tq):
    h = pl.program_id(1)
    n_sub = qt_ref.shape[-1] // tq
    ones = jnp.ones((SUM_ROWS, kw), BF16)

    def tile(j, small):
        t0 = (pl.program_id(2) * n_sub + j) * tq
        start = pl.multiple_of(jnp.clip(t0 - WINDOW, 0, seq - kw), LANES)
        k = k_ref[0, 0, pl.ds(start, kw), :]
        vt = vt_ref[0, 0, :, pl.ds(start, kw)]
        spos = start + lax.broadcasted_iota(jnp.int32, (kw, tq), 0)
        tpos = t0 + lax.broadcasted_iota(jnp.int32, (kw, tq), 1)
        dist = jnp.abs(tpos - spos)
        valid = dist <= WINDOW
        distf = dist.astype(F32)
        scores = [jnp.dot(k, qt_ref[0, g, :, j * tq:(j + 1) * tq], preferred_element_type=F32)
                  for g in range(GROUP)]
        outs = []
        if small:
            far = jnp.where(valid, distf, jnp.inf)
            vt_ext = jnp.concatenate([vt, ones], axis=0)
        for g in range(GROUP):
            slope = hp_ref[0, h * GROUP + g]
            sink = hp_ref[1, h * GROUP + g]
            if small:
                p = jnp.exp2(scores[g] - slope * far).astype(BF16)
                acc = jnp.dot(vt_ext, p, preferred_element_type=F32)
                denom = acc[HEAD_DIM:HEAD_DIM + 1] + jnp.exp2(jnp.full((1, tq), sink, F32))
                outs.append(acc[:HEAD_DIM] / denom)
            else:
                s = jnp.where(valid, scores[g] - slope * distf, -jnp.inf)
                m = jnp.maximum(jnp.max(s, axis=0, keepdims=True), sink)
                p = jnp.exp2(s - m)
                denom = jnp.sum(p, axis=0, keepdims=True) + jnp.exp2(sink - m)
                outs.append(jnp.dot(vt, p.astype(BF16), preferred_element_type=F32) / denom)
        o_ref[0, j * tq:(j + 1) * tq, :] = jnp.concatenate(outs, axis=0).T

    def run(small):
        for j in range(n_sub):
            tile(j, small)

    small = jnp.logical_and(hp_ref[2, 0] <= B_SCORE_SAFE, hp_ref[2, 1] <= B_SINK_SAFE)
    pl.when(small)(lambda: run(True))
    pl.when(jnp.logical_not(small))(lambda: run(False))


def _attn_b(head_params, qt, k, vt):
    B, _, _, S = qt.shape
    tq = _tile(S, 256)
    kw = min(S, tq + 2 * WINDOW)
    tqb = _tile(S, B_TILES_PER_STEP * tq)
    return pl.pallas_call(
        functools.partial(_attn_b_kernel, seq=S, kw=kw, tq=tq),
        grid=(B, N_KV, S // tqb),
        in_specs=[
            pl.BlockSpec(memory_space=pltpu.SMEM),
            pl.BlockSpec((1, GROUP, HEAD_DIM, tqb), lambda b, h, i: (b, h, 0, i)),
            pl.BlockSpec((1, 1, S, HEAD_DIM), lambda b, h, i: (b, h, 0, 0)),
            pl.BlockSpec((1, 1, HEAD_DIM, S), lambda b, h, i: (b, h, 0, 0)),
        ],
        out_specs=pl.BlockSpec((1, tqb, GROUP * HEAD_DIM), lambda b, h, i: (b, i, h)),
        out_shape=jax.ShapeDtypeStruct((B, S, WIDTH), F32),
        compiler_params=pltpu.CompilerParams(
            dimension_semantics=("parallel", "parallel", "parallel"), vmem_limit_bytes=VMEM_LIMIT),
        name="attn_b",
    )(head_params, qt, k, vt)


def _outproj_kernel(ya_ref, yb_ref, x_ref, ga_ref, gb_ref, w_ref, o_ref):
    na = _rms(ya_ref[0], ga_ref[...]).astype(BF16)
    nb = _rms(yb_ref[0], gb_ref[...]).astype(BF16)
    o_ref[0] = (x_ref[0]
                + jnp.dot(na, w_ref[:WIDTH], preferred_element_type=F32)
                + jnp.dot(nb, w_ref[WIDTH:], preferred_element_type=F32))


def _outproj(ya, yb, x, ga, gb, w_out):
    B, S, D = x.shape
    T = _tile(S, 512)
    const = lambda b, i: (0, 0)
    row = lambda b, i: (b, i, 0)
    return pl.pallas_call(
        _outproj_kernel,
        grid=(B, S // T),
        in_specs=[
            pl.BlockSpec((1, T, WIDTH), row),
            pl.BlockSpec((1, T, WIDTH), row),
            pl.BlockSpec((1, T, D), row),
            pl.BlockSpec((1, WIDTH), const),
            pl.BlockSpec((1, WIDTH), const),
            pl.BlockSpec((2 * WIDTH, D), const),
        ],
        out_specs=pl.BlockSpec((1, T, D), row),
        out_shape=jax.ShapeDtypeStruct((B, S, D), F32),
        compiler_params=pltpu.CompilerParams(
            dimension_semantics=("parallel", "parallel"), vmem_limit_bytes=VMEM_LIMIT),
        name="outproj",
    )(ya, yb, x, ga, gb, w_out)


def _gelu_tanh(x):
    c = math.sqrt(2.0 / math.pi)
    hx = 0.5 * x
    return hx + hx * jnp.tanh(x * (c + (c * 0.044715) * (x * x)))


def _ffn_kernel(x_ref, prev_ref, next_ref, g_ref, wup_ref, cw_ref, cb_ref, wdn_ref, o_ref,
                h_ref, u_ref, act_ref):
    i = pl.program_id(1)
    T = x_ref.shape[1]
    g = g_ref[...]
    h_prev = _rms(prev_ref[0], g) * (i > 0).astype(F32)
    h_next = _rms(next_ref[0], g) * (i < pl.num_programs(1) - 1).astype(F32)
    h_ref[:HALO] = h_prev.astype(BF16)
    h_ref[HALO:HALO + T] = _rms(x_ref[0], g).astype(BF16)
    h_ref[HALO + T:] = h_next.astype(BF16)
    h = h_ref[...]

    for j in range(D_FF // FF_CHUNK):
        slot = j % U_SLOTS
        cols = (j * FF_CHUNK, D_FF + j * FF_CHUNK)
        for part, lo in enumerate(cols):
            u_ref[slot, part] = jnp.dot(h, wup_ref[:, lo:lo + FF_CHUNK], preferred_element_type=F32)

        def conv(part, lo):
            w = cw_ref[:, lo:lo + FF_CHUNK]
            return (u_ref[slot, part, HALO - 1:HALO - 1 + T] * w[0:1]
                    + u_ref[slot, part, HALO:HALO + T] * w[1:2]
                    + u_ref[slot, part, HALO + 1:HALO + 1 + T] * w[2:3]
                    + cb_ref[:, lo:lo + FF_CHUNK])

        act = _gelu_tanh(conv(0, cols[0])) * conv(1, cols[1])
        act_ref[:, cols[0]:cols[0] + FF_CHUNK] = act.astype(BF16)
    o_ref[0] = x_ref[0] + jnp.dot(act_ref[...], wdn_ref[...], preferred_element_type=F32)


def _ffn(x, g, w_up, conv_w, conv_b, w_down):
    B, S, D = x.shape
    T = _tile(S, 512)
    hb = T // HALO
    n_hb = S // HALO
    const = lambda b, i: (0, 0)
    return pl.pallas_call(
        _ffn_kernel,
        grid=(B, S // T),
        in_specs=[
            pl.BlockSpec((1, T, D), lambda b, i: (b, i, 0)),
            pl.BlockSpec((1, HALO, D), lambda b, i: (b, jnp.maximum(i * hb - 1, 0), 0)),
            pl.BlockSpec((1, HALO, D), lambda b, i: (b, jnp.minimum((i + 1) * hb, n_hb - 1), 0)),
            pl.BlockSpec((1, D), const),
            pl.BlockSpec((D, 2 * D_FF), const),
            pl.BlockSpec((3, 2 * D_FF), const),
            pl.BlockSpec((1, 2 * D_FF), const),
            pl.BlockSpec((D_FF, D), const),
        ],
        out_specs=pl.BlockSpec((1, T, D), lambda b, i: (b, i, 0)),
        out_shape=jax.ShapeDtypeStruct((B, S, D), F32),
        scratch_shapes=[pltpu.VMEM((T + 2 * HALO, D), BF16),
                        pltpu.VMEM((U_SLOTS, 2, T + 2 * HALO, FF_CHUNK), F32),
                        pltpu.VMEM((T, D_FF), BF16)],
        compiler_params=pltpu.CompilerParams(
            dimension_semantics=("parallel", "parallel"), vmem_limit_bytes=VMEM_LIMIT),
        name="ffn",
    )(x, x, x, g, w_up, conv_w, conv_b, w_down)


def _rope_tables(S):
    t = jnp.arange(S, dtype=jnp.int32)
    pos = jnp.stack([(t // GRID_W).astype(F32), (t % GRID_W).astype(F32)], axis=1)
    axis_dim = HEAD_DIM // 2
    inv = ROPE_THETA ** (-jnp.arange(0, axis_dim, 2, dtype=F32) / axis_dim)
    ang = pos[:, :, None] * inv
    cos = jnp.cos(ang)
    sin = jnp.sin(ang)
    cos_h = jnp.concatenate([cos, cos], axis=-1).reshape(S, HEAD_DIM)
    sin_h = jnp.concatenate([-sin, sin], axis=-1).reshape(S, HEAD_DIM)
    reps = LANES // HEAD_DIM
    return jnp.tile(cos_h, (1, reps)), jnp.tile(sin_h, (1, reps))


def _score_bound(gq, gk):
    bound = HEAD_DIM * jnp.max(jnp.abs(gq)) * jnp.max(jnp.abs(gk)) * Q_SCALE * (1.0 + 2.0 ** -7) ** 2
    return bound.astype(F32).reshape(1)


def _attn_a_flags(gq, gk):
    root = math.sqrt(HEAD_DIM)
    q_max = root * jnp.max(jnp.abs(gq)) * (Q_SCALE * 2.0 ** Q8_SHIFT * 1.02)
    k_max = root * jnp.max(jnp.abs(gk)) * (2.0 ** K8_SHIFT * 1.02)
    fits = jnp.logical_and(q_max <= F8_MAX, k_max <= F8_MAX).astype(F32)
    return jnp.concatenate([_score_bound(gq, gk), fits.reshape(1)])


def _head_params_b(slopes, sink, gq, gk):
    sink2 = sink.astype(F32) * LOG2E
    limits = jnp.zeros((N_HEADS,), F32).at[0].set(_score_bound(gq, gk)[0]).at[1].set(jnp.max(jnp.abs(sink2)))
    return jnp.stack([slopes * LOG2E, sink2, limits])


def _layer(x, p):
    S = x.shape[1]
    cos, sin = _rope_tables(S)
    qa, ka, va, qb, kb, vb, qa8, ka8 = _proj(x, p["norm_mix_g"], p["w_in"], p["head_gain"], cos, sin)
    ya = _attn_a(p["score_bound_a"], qa, ka, va, qa8, ka8)
    yb = _attn_b(p["head_params"], qb, kb, vb)
    x1 = _outproj(ya, yb, x, p["out_norm_a_g"], p["out_norm_b_g"], p["w_out"])
    return _ffn(x1, p["norm_ffn_g"], p["w_up"], p["conv_w"], p["conv_b"], p["w_down"])


def kernel(x_prompt, x_sample, norm_mix_g, w_in, qnorm_a_g, knorm_a_g, qnorm_b_g, knorm_b_g,
           sink_b, out_norm_a_g, out_norm_b_g, w_out, norm_ffn_g, w_up, conv_w, conv_b, w_down):
    depth = w_in.shape[0]
    y_prompt, y_sample = x_prompt, x_sample
    slopes = jnp.exp2(-8.0 * jnp.arange(1, N_HEADS + 1, dtype=F32) / N_HEADS)
    ones_kv = jnp.ones((KV_WIDTH,), F32)
    for l in range(depth):
        head_gain = jnp.concatenate([
            jnp.tile(qnorm_a_g[l], N_HEADS), jnp.tile(knorm_a_g[l], N_KV), ones_kv,
            jnp.tile(qnorm_b_g[l], N_HEADS), jnp.tile(knorm_b_g[l], N_KV), ones_kv])[None]
        p = {
            "norm_mix_g": norm_mix_g[l][None],
            "w_in": w_in[l].astype(BF16),
            "head_gain": head_gain,
            "score_bound_a": _attn_a_flags(qnorm_a_g[l], knorm_a_g[l]),
            "head_params": _head_params_b(slopes, sink_b[l], qnorm_b_g[l], knorm_b_g[l]),
            "out_norm_a_g": out_norm_a_g[l][None],
            "out_norm_b_g": out_norm_b_g[l][None],
            "w_out": w_out[l].astype(BF16),
            "norm_ffn_g": norm_ffn_g[l][None],
            "w_up": w_up[l].astype(BF16),
            "conv_w": conv_w[l],
            "conv_b": conv_b[l][None],
            "w_down": w_down[l].astype(BF16),
        }
        y_prompt = _layer(y_prompt, p)
        y_sample = _layer(y_sample, p)
    return (y_prompt, y_sample)
```

```python
import functools
import math

import jax
import jax.numpy as jnp
from jax import lax
from jax.experimental import pallas as pl
from jax.experimental.pallas import tpu as pltpu

D_MODEL = 1024
HEAD_DIM = 64
N_HEADS = 8
N_KV = 2
GROUP = N_HEADS // N_KV
WIDTH = N_HEADS * HEAD_DIM
KV_WIDTH = N_KV * HEAD_DIM
IN_WIDTH = 2 * (WIDTH + 2 * KV_WIDTH)
WINDOW = 128
GRID_W = 64
ROPE_THETA = 10000.0
D_FF = 2816
EPS = 1e-6
LOG2E = math.log2(math.e)
Q_SCALE = HEAD_DIM ** -0.5 * LOG2E

LANES = 128
NORM_CHUNK = 256
BF16_ROWS = 16
HALO = BF16_ROWS
SUM_ROWS = BF16_ROWS
AHEAD = 1
CHUNKS_PER_TRIP = 8
FF_CHUNK = 256
U_SLOTS = 3
EXP2_SAFE_RANGE = 50.0
B_TILES_PER_STEP = 4
B_SCORE_SAFE = 30.0
B_SINK_SAFE = 60.0
VMEM_LIMIT =56 * 1024 * 1024

F32 = jnp.float32
BF16 = jnp.bfloat16
F8 = jnp.float8_e4m3fn
F8_MAX = 448.0
Q8_SHIFT = 6
K8_SHIFT = 4
STACK = 4 * HEAD_DIM


def _tile(n, want):
    t = min(n, want)
    assert n % t == 0, (n, t)
    return t


def _rms(x, g):
    return x * lax.rsqrt(jnp.mean(x * x, axis=-1, keepdims=True) + EPS) * g


def _proj_kernel(x_ref, g_ref, w_ref, hg_ref, cos_ref, sin_ref,
                 qa_ref, ka_ref, va_ref, qb_ref, kb_ref, vb_ref, qa8_ref, ka8_ref):
    h = _rms(x_ref[0], g_ref[...]).astype(BF16)
    r = lax.broadcasted_iota(jnp.int32, (NORM_CHUNK, NORM_CHUNK), 0) // HEAD_DIM
    c = lax.broadcasted_iota(jnp.int32, (NORM_CHUNK, NORM_CHUNK), 1) // HEAD_DIM
    seg = (r == c).astype(BF16)

    def project(lo):
        return jnp.dot(h, w_ref[:, lo:lo + NORM_CHUNK], preferred_element_type=F32)

    def head_norm(v, lo):
        sq = v * v
        sq_hi = sq.astype(BF16)
        sq_lo = (sq - sq_hi.astype(F32)).astype(BF16)
        ss = (jnp.dot(sq_hi, seg, preferred_element_type=F32)
              + jnp.dot(sq_lo, seg, preferred_element_type=F32))
        return v * lax.rsqrt(ss * (1.0 / HEAD_DIM) + EPS) * hg_ref[:, lo:lo + NORM_CHUNK]

    lane = lax.broadcasted_iota(jnp.int32, (1, LANES), 1)
    first_half = (lane % 32) < 16
    cos = cos_ref[...]
    sin = sin_ref[...]

    def rope(v):
        partner = jnp.where(first_half, pltpu.roll(v, LANES - 16, 1), pltpu.roll(v, 16, 1))
        return v * cos + partner * sin

    def put_t(ref, first_head, v):
        vt = v.T
        ref[0, first_head] = vt[:HEAD_DIM].astype(ref.dtype)
        ref[0, first_head + 1] = vt[HEAD_DIM:].astype(ref.dtype)
        return vt

    def split8(x):
        hi = x.astype(F8)
        return hi, (x - hi.astype(F32)).astype(F8)

    def put_q8(first_head, vt):
        scaled = vt * (2.0 ** Q8_SHIFT)
        for i in range(LANES // HEAD_DIM):
            hi, lo = split8(scaled[i * HEAD_DIM:(i + 1) * HEAD_DIM])
            qa8_ref[0, first_head + i] = jnp.concatenate([hi, lo, hi, lo], axis=0)

    def put_k8(v):
        scaled = v * (2.0 ** K8_SHIFT)
        swapped = pltpu.roll(scaled, HEAD_DIM, 1)
        low = lane < HEAD_DIM
        for i, dup in enumerate((jnp.where(low, scaled, swapped), jnp.where(low, swapped, scaled))):
            hi, lo = split8(dup)
            ka8_ref[0, i] = jnp.concatenate([hi, lo], axis=1)

    def put(ref, v):
        ref[0, 0] = v[:, :HEAD_DIM].astype(ref.dtype)
        ref[0, 1] = v[:, HEAD_DIM:].astype(ref.dtype)

    def finish_q(v, lo, ref, first_head, rotary):
        q = head_norm(v, lo)
        for p in range(NORM_CHUNK // LANES):
            part = q[:, p * LANES:(p + 1) * LANES]
            hd = first_head + p * (LANES // HEAD_DIM)
            vt = put_t(ref, hd, (rope(part) if rotary else part) * Q_SCALE)
            if rotary:
                put_q8(hd, vt)

    def finish_kv(v, lo, k_ref, v_ref, rotary):
        k = head_norm(v, lo)[:, :LANES]
        k = rope(k) if rotary else k
        put(k_ref, k)
        if rotary:
            put_k8(k)
        put_t(v_ref, 0, v[:, LANES:])

    heads_per_chunk = NORM_CHUNK // HEAD_DIM
    o_qb = WIDTH + 2 * KV_WIDTH
    stages = []
    for j in range(WIDTH // NORM_CHUNK):
        stages.append((j * NORM_CHUNK, functools.partial(
            finish_q, ref=qa_ref, first_head=j * heads_per_chunk, rotary=True)))
        stages.append((o_qb + j * NORM_CHUNK, functools.partial(
            finish_q, ref=qb_ref, first_head=j * heads_per_chunk, rotary=False)))
    stages.append((WIDTH, functools.partial(finish_kv, k_ref=ka_ref, v_ref=va_ref, rotary=True)))
    stages.append((o_qb + WIDTH, functools.partial(finish_kv, k_ref=kb_ref, v_ref=vb_ref, rotary=False)))

    depth = AHEAD + 1
    pending = [project(lo) for lo, _ in stages[:depth]]
    for n, (lo, finish) in enumerate(stages):
        if n + depth < len(stages):
            pending.append(project(stages[n + depth][0]))
        finish(pending[n], lo)


def _proj(x, g, w_in, head_gain, cos, sin):
    B, S, D = x.shape
    T = _tile(S, 512)
    qt_shape = jax.ShapeDtypeStruct((B, N_HEADS, HEAD_DIM, S), BF16)
    k_shape = jax.ShapeDtypeStruct((B, N_KV, S, HEAD_DIM), BF16)
    vt_shape = jax.ShapeDtypeStruct((B, N_KV, HEAD_DIM, S), BF16)
    q8_shape = jax.ShapeDtypeStruct((B, N_HEADS, STACK, S), F8)
    k8_shape = jax.ShapeDtypeStruct((B, N_KV, S, STACK), F8)
    q8_spec = pl.BlockSpec((1, N_HEADS, STACK, T), lambda b, i: (b, 0, 0, i))
    k8_spec = pl.BlockSpec((1, N_KV, T, STACK), lambda b, i: (b, 0, i, 0))
    qt_spec = pl.BlockSpec((1, N_HEADS, HEAD_DIM, T), lambda b, i: (b, 0, 0, i))
    k_spec = pl.BlockSpec((1, N_KV, T, HEAD_DIM), lambda b, i: (b, 0, i, 0))
    vt_spec = pl.BlockSpec((1, N_KV, HEAD_DIM, T), lambda b, i: (b, 0, 0, i))
    const = lambda b, i: (0, 0)
    return pl.pallas_call(
        _proj_kernel,
        grid=(B, S // T),
        in_specs=[
            pl.BlockSpec((1, T, D), lambda b, i: (b, i, 0)),
            pl.BlockSpec((1, D), const),
            pl.BlockSpec((D, IN_WIDTH), const),
            pl.BlockSpec((1, IN_WIDTH), const),
            pl.BlockSpec((T, LANES), lambda b, i: (i, 0)),
            pl.BlockSpec((T, LANES), lambda b, i: (i, 0)),
        ],
        out_specs=[qt_spec, k_spec, vt_spec, qt_spec, k_spec, vt_spec, q8_spec, k8_spec],
        out_shape=[qt_shape, k_shape, vt_shape, qt_shape, k_shape, vt_shape, q8_shape, k8_shape],
        compiler_params=pltpu.CompilerParams(
            dimension_semantics=("parallel", "parallel"), vmem_limit_bytes=VMEM_LIMIT),
        name="proj",
    )(x, g, w_in, head_gain, cos, sin)


def _attn_a_kernel(bound_ref, qt_ref, k_ref, vt_ref, qt8_ref, k8_ref, o_ref, s_ref, *, tk, nk, cpb):
    tq = qt_ref.shape[-1]
    ones = jnp.ones((SUM_ROWS, tk), BF16)

    def scores(i, g, stacked8):
        ks = i * tk if isinstance(i, int) else pl.multiple_of(i * tk, tk)
        if stacked8:
            s = jnp.dot(k8_ref[0, 0, pl.ds(ks, tk), :], qt8_ref[0, g], preferred_element_type=F32)
            return s * (2.0 ** -(Q8_SHIFT + K8_SHIFT))
        return jnp.dot(k_ref[0, 0, pl.ds(ks, tk), :], qt_ref[0, g], preferred_element_type=F32)

    def item(w):
        return w // GROUP, w % GROUP

    def run(online, stacked8):
        for w in range(AHEAD):
            s_ref[w] = scores(*item(w), stacked8)

        def body(j, carry):
            c0 = j * cpb
            state = list(carry)
            tiles = [s_ref[w] for w in range(AHEAD)]
            for u in range(cpb * GROUP):
                dc, g_next = item(u + AHEAD)
                tiles.append(scores(jnp.minimum(c0 + dc, nk - 1), g_next, stacked8))
                dc, g = item(u)
                ks = pl.multiple_of((c0 + dc) * tk, tk)
                vt = jnp.concatenate([vt_ref[0, 0, :, pl.ds(ks, tk)], ones], axis=0)
                s = tiles[u]
                if online:
                    m, acc = state[g]
                    m_new = jnp.maximum(m, jnp.max(s, axis=0, keepdims=True))
                    p = jnp.exp2(s - m_new).astype(BF16)
                    acc = jnp.exp2(m - m_new) * acc + jnp.dot(vt, p, preferred_element_type=F32)
                    state[g] = (m_new, acc)
                else:
                    p = jnp.exp2(s)
                    l, acc = state[g]
                    state[g] = (l + jnp.sum(p, axis=0, keepdims=True),
                                acc + jnp.dot(vt[:HEAD_DIM], p.astype(BF16), preferred_element_type=F32))
            for w in range(AHEAD):
                s_ref[w] = tiles[cpb * GROUP + w]
            return tuple(state)

        acc0 = jnp.zeros((HEAD_DIM + SUM_ROWS, tq), F32)
        if online:
            init = tuple((jnp.full((1, tq), -jnp.inf, F32), acc0) for _ in range(GROUP))
            accs = [acc for (_, acc) in lax.fori_loop(0, nk // cpb, body, init)]
        else:
            init = tuple((jnp.zeros((1, tq), F32), jnp.zeros((HEAD_DIM, tq), F32)) for _ in range(GROUP))
            accs = [jnp.concatenate([acc, l], axis=0) for (l, acc) in lax.fori_loop(0, nk // cpb, body, init)]
        o = jnp.concatenate([acc[:HEAD_DIM] / acc[HEAD_DIM:HEAD_DIM + 1] for acc in accs], axis=0)
        o_ref[0] = o.T

    small = bound_ref[0] <= EXP2_SAFE_RANGE
    use8 = jnp.logical_and(small, bound_ref[1] > 0.5)
    pl.when(use8)(lambda: run(False, True))
    pl.when(jnp.logical_and(small, jnp.logical_not(use8)))(lambda: run(False, False))
    pl.when(jnp.logical_not(small))(lambda: run(True, False))


def _attn_a(score_bound, qt, k, vt, qt8, k8):
    B, _, _, S = qt.shape
    tq = _tile(S, 512)
    tk = _tile(S, 512)
    nk = S // tk
    cpb = math.gcd(nk, CHUNKS_PER_TRIP)
    return pl.pallas_call(
        functools.partial(_attn_a_kernel, tk=tk, nk=nk, cpb=cpb),
        grid=(B, N_KV, S // tq),
        in_specs=[
            pl.BlockSpec(memory_space=pltpu.SMEM),
            pl.BlockSpec((1, GROUP, HEAD_DIM, tq), lambda b, h, i: (b, h, 0, i)),
            pl.BlockSpec((1, 1, S, HEAD_DIM), lambda b, h, i: (b, h, 0, 0)),
            pl.BlockSpec((1, 1, HEAD_DIM, S), lambda b, h, i: (b, h, 0, 0)),
            pl.BlockSpec((1, GROUP, STACK, tq), lambda b, h, i: (b, h, 0, i)),
            pl.BlockSpec((1, 1, S, STACK), lambda b, h, i: (b, h, 0, 0)),
        ],
        out_specs=pl.BlockSpec((1, tq, GROUP * HEAD_DIM), lambda b, h, i: (b, i, h)),
        out_shape=jax.ShapeDtypeStruct((B, S, WIDTH), F32),
        scratch_shapes=[pltpu.VMEM((AHEAD, tk, tq), F32)],
        compiler_params=pltpu.CompilerParams(
            dimension_semantics=("parallel", "parallel", "parallel"), vmem_limit_bytes=VMEM_LIMIT),
        name="attn_a",
    )(score_bound, qt, k, vt, qt8, k8)


def _attn_b_kernel(hp_ref, qt_ref, k_ref, vt_ref, o_ref, *, seq, kw, tq):
    h = pl.program_id(1)
    n_sub = qt_ref.shape[-1] // tq
    ones = jnp.ones((SUM_ROWS, kw), BF16)

    def tile(j, small):
        t0 = (pl.program_id(2) * n_sub + j) * tq
        start = pl.multiple_of(jnp.clip(t0 - WINDOW, 0, seq - kw), LANES)
        k = k_ref[0, 0, pl.ds(start, kw), :]
        vt = vt_ref[0, 0, :, pl.ds(start, kw)]
        spos = start + lax.broadcasted_iota(jnp.int32, (kw, tq), 0)
        tpos = t0 + lax.broadcasted_iota(jnp.int32, (kw, tq), 1)
        dist = jnp.abs(tpos - spos)
        valid = dist <= WINDOW
        distf = dist.astype(F32)
        scores = [jnp.dot(k, qt_ref[0, g, :, j * tq:(j + 1) * tq], preferred_element_type=F32)
                  for g in range(GROUP)]
        outs = []
        if small:
            far = jnp.where(valid, distf, jnp.inf)
            vt_ext = jnp.concatenate([vt, ones], axis=0)
        for g in range(GROUP):
            slope = hp_ref[0, h * GROUP + g]
            sink = hp_ref[1, h * GROUP + g]
            if small:
                p = jnp.exp2(scores[g] - slope * far).astype(BF16)
                acc = jnp.dot(vt_ext, p, preferred_element_type=F32)
                denom = acc[HEAD_DIM:HEAD_DIM + 1] + jnp.exp2(jnp.full((1, tq), sink, F32))
                outs.append(acc[:HEAD_DIM] / denom)
            else:
                s = jnp.where(valid, scores[g] - slope * distf, -jnp.inf)
                m = jnp.maximum(jnp.max(s, axis=0, keepdims=True), sink)
                p = jnp.exp2(s - m)
                denom = jnp.sum(p, axis=0, keepdims=True) + jnp.exp2(sink - m)
                outs.append(jnp.dot(vt, p.astype(BF16), preferred_element_type=F32) / denom)
        o_ref[0, j * tq:(j + 1) * tq, :] = jnp.concatenate(outs, axis=0).T

    def run(small):
        for j in range(n_sub):
            tile(j, small)

    small = jnp.logical_and(hp_ref[2, 0] <= B_SCORE_SAFE, hp_ref[2, 1] <= B_SINK_SAFE)
    pl.when(small)(lambda: run(True))
    pl.when(jnp.logical_not(small))(lambda: run(False))


def _attn_b(head_params, qt, k, vt):
    B, _, _, S = qt.shape
    tq = _tile(S, 256)
    kw = min(S, tq + 2 * WINDOW)
    tqb = _tile(S, B_TILES_PER_STEP * tq)
    return pl.pallas_call(
        functools.partial(_attn_b_kernel, seq=S, kw=kw, tq=tq),
        grid=(B, N_KV, S // tqb),
        in_specs=[
            pl.BlockSpec(memory_space=pltpu.SMEM),
            pl.BlockSpec((1, GROUP, HEAD_DIM, tqb), lambda b, h, i: (b, h, 0, i)),
            pl.BlockSpec((1, 1, S, HEAD_DIM), lambda b, h, i: (b, h, 0, 0)),
            pl.BlockSpec((1, 1, HEAD_DIM, S), lambda b, h, i: (b, h, 0, 0)),
        ],
        out_specs=pl.BlockSpec((1, tqb, GROUP * HEAD_DIM), lambda b, h, i: (b, i, h)),
        out_shape=jax.ShapeDtypeStruct((B, S, WIDTH), F32),
        compiler_params=pltpu.CompilerParams(
            dimension_semantics=("parallel", "parallel", "parallel"), vmem_limit_bytes=VMEM_LIMIT),
        name="attn_b",
    )(head_params, qt, k, vt)


def _gelu_tanh(x):
    c = math.sqrt(2.0 / math.pi)
    hx = 0.5 * x
    return hx + hx * jnp.tanh(x * (c + (c * 0.044715) * (x * x)))


def _mix_ffn_kernel(ya_ref, ya_prev, ya_next, yb_ref, yb_prev, yb_next, x_ref, x_prev, x_next,
                    ga_ref, gb_ref, wout_ref, g_ref, wup_ref, cw_ref, cb_ref, wdn_ref, o_ref,
                    n_ref, h_ref, u_ref, act_ref):
    i = pl.program_id(1)
    T = x_ref.shape[1]
    tile = slice(HALO, HALO + T)

    for part, (gain, refs) in enumerate(((ga_ref, (ya_prev, ya_ref, ya_next)),
                                         (gb_ref, (yb_prev, yb_ref, yb_next)))):
        lo = 0
        for r in refs:
            rows = r.shape[1]
            n_ref[part, lo:lo + rows] = _rms(r[0], gain[...]).astype(BF16)
            lo += rows
    mixed = (jnp.dot(n_ref[0], wout_ref[:WIDTH], preferred_element_type=F32)
             + jnp.dot(n_ref[1], wout_ref[WIDTH:], preferred_element_type=F32))
    o_ref[0] = x_ref[0] + mixed[tile]

    g = g_ref[...]
    h_prev = _rms(x_prev[0] + mixed[:HALO], g) * (i > 0).astype(F32)
    h_next = _rms(x_next[0] + mixed[HALO + T:], g) * (i < pl.num_programs(1) - 1).astype(F32)
    h_ref[:HALO] = h_prev.astype(BF16)
    h_ref[tile] = _rms(o_ref[0], g).astype(BF16)
    h_ref[HALO + T:] = h_next.astype(BF16)
    h = h_ref[...]

    for j in range(D_FF // FF_CHUNK):
        slot = j % U_SLOTS
        cols = (j * FF_CHUNK, D_FF + j * FF_CHUNK)
        for part, lo in enumerate(cols):
            u_ref[slot, part] = jnp.dot(h, wup_ref[:, lo:lo + FF_CHUNK], preferred_element_type=F32)

        def conv(part, lo):
            w = cw_ref[:, lo:lo + FF_CHUNK]
            return (u_ref[slot, part, HALO - 1:HALO - 1 + T] * w[0:1]
                    + u_ref[slot, part, HALO:HALO + T] * w[1:2]
                    + u_ref[slot, part, HALO + 1:HALO + 1 + T] * w[2:3]
                    + cb_ref[:, lo:lo + FF_CHUNK])

        act = _gelu_tanh(conv(0, cols[0])) * conv(1, cols[1])
        act_ref[:, cols[0]:cols[0] + FF_CHUNK] = act.astype(BF16)
    o_ref[0] = o_ref[0] + jnp.dot(act_ref[...], wdn_ref[...], preferred_element_type=F32)


def _mix_ffn(ya, yb, x, ga, gb, w_out, g, w_up, conv_w, conv_b, w_down):
    B, S, D = x.shape
    T = _tile(S, 512)
    hb = T // HALO
    n_hb = S // HALO
    const = lambda b, i: (0, 0)
    resident = functools.partial(pl.BlockSpec, index_map=const, pipeline_mode=pl.Buffered(1))

    def rows(width):
        return [pl.BlockSpec((1, T, width), lambda b, i: (b, i, 0)),
                pl.BlockSpec((1, HALO, width), lambda b, i: (b, jnp.maximum(i * hb - 1, 0), 0)),
                pl.BlockSpec((1, HALO, width), lambda b, i: (b, jnp.minimum((i + 1) * hb, n_hb - 1), 0))]

    return pl.pallas_call(
        _mix_ffn_kernel,
        grid=(B, S // T),
        in_specs=rows(WIDTH) + rows(WIDTH) + rows(D) + [
            pl.BlockSpec((1, WIDTH), const),
            pl.BlockSpec((1, WIDTH), const),
            resident((2 * WIDTH, D)),
            pl.BlockSpec((1, D), const),
            resident((D, 2 * D_FF)),
            pl.BlockSpec((3, 2 * D_FF), const),
            pl.BlockSpec((1, 2 * D_FF), const),
            resident((D_FF, D)),
        ],
        out_specs=pl.BlockSpec((1, T, D), lambda b, i: (b, i, 0)),
        out_shape=jax.ShapeDtypeStruct((B, S, D), F32),
        scratch_shapes=[pltpu.VMEM((2, T + 2 * HALO, WIDTH), BF16),
                        pltpu.VMEM((T + 2 * HALO, D), BF16),
                        pltpu.VMEM((U_SLOTS, 2, T + 2 * HALO, FF_CHUNK), F32),
                        pltpu.VMEM((T, D_FF), BF16)],
        compiler_params=pltpu.CompilerParams(
            dimension_semantics=("parallel", "parallel"), vmem_limit_bytes=VMEM_LIMIT),
        name="mix_ffn",
    )(ya, ya, ya, yb, yb, yb, x, x, x, ga, gb, w_out, g, w_up, conv_w, conv_b, w_down)


def _rope_tables(S):
    t = jnp.arange(S, dtype=jnp.int32)
    pos = jnp.stack([(t // GRID_W).astype(F32), (t % GRID_W).astype(F32)], axis=1)
    axis_dim = HEAD_DIM // 2
    inv = ROPE_THETA ** (-jnp.arange(0, axis_dim, 2, dtype=F32) / axis_dim)
    ang = pos[:, :, None] * inv
    cos = jnp.cos(ang)
    sin = jnp.sin(ang)
    cos_h = jnp.concatenate([cos, cos], axis=-1).reshape(S, HEAD_DIM)
    sin_h = jnp.concatenate([-sin, sin], axis=-1).reshape(S, HEAD_DIM)
    reps = LANES // HEAD_DIM
    return jnp.tile(cos_h, (1, reps)), jnp.tile(sin_h, (1, reps))


def _score_bound(gq, gk):
    bound = HEAD_DIM * jnp.max(jnp.abs(gq)) * jnp.max(jnp.abs(gk)) * Q_SCALE * (1.0 + 2.0 ** -7) ** 2
    return bound.astype(F32).reshape(1)


def _attn_a_flags(gq, gk):
    root = math.sqrt(HEAD_DIM)
    q_max = root * jnp.max(jnp.abs(gq)) * (Q_SCALE * 2.0 ** Q8_SHIFT * 1.02)
    k_max = root * jnp.max(jnp.abs(gk)) * (2.0 ** K8_SHIFT * 1.02)
    fits = jnp.logical_and(q_max <= F8_MAX, k_max <= F8_MAX).astype(F32)
    return jnp.concatenate([_score_bound(gq, gk), fits.reshape(1)])


def _head_params_b(slopes, sink, gq, gk):
    sink2 = sink.astype(F32) * LOG2E
    limits = jnp.zeros((N_HEADS,), F32).at[0].set(_score_bound(gq, gk)[0]).at[1].set(jnp.max(jnp.abs(sink2)))
    return jnp.stack([slopes * LOG2E, sink2, limits])


def _layer(x, p):
    S = x.shape[1]
    cos, sin = _rope_tables(S)
    qa, ka, va, qb, kb, vb, qa8, ka8 = _proj(x, p["norm_mix_g"], p["w_in"], p["head_gain"], cos, sin)
    ya = _attn_a(p["score_bound_a"], qa, ka, va, qa8, ka8)
    yb = _attn_b(p["head_params"], qb, kb, vb)
    return _mix_ffn(ya, yb, x, p["out_norm_a_g"], p["out_norm_b_g"], p["w_out"],
                    p["norm_ffn_g"], p["w_up"], p["conv_w"], p["conv_b"], p["w_down"])


def kernel(x_prompt, x_sample, norm_mix_g, w_in, qnorm_a_g, knorm_a_g, qnorm_b_g, knorm_b_g,
           sink_b, out_norm_a_g, out_norm_b_g, w_out, norm_ffn_g, w_up, conv_w, conv_b, w_down):
    depth = w_in.shape[0]
    y_prompt, y_sample = x_prompt, x_sample
    slopes = jnp.exp2(-8.0 * jnp.arange(1, N_HEADS + 1, dtype=F32) / N_HEADS)
    ones_kv = jnp.ones((KV_WIDTH,), F32)
    for l in range(depth):
        head_gain = jnp.concatenate([
            jnp.tile(qnorm_a_g[l], N_HEADS), jnp.tile(knorm_a_g[l], N_KV), ones_kv,
            jnp.tile(qnorm_b_g[l], N_HEADS), jnp.tile(knorm_b_g[l], N_KV), ones_kv])[None]
        p = {
            "norm_mix_g": norm_mix_g[l][None],
            "w_in": w_in[l].astype(BF16),
            "head_gain": head_gain,
            "score_bound_a": _attn_a_flags(qnorm_a_g[l], knorm_a_g[l]),
            "head_params": _head_params_b(slopes, sink_b[l], qnorm_b_g[l], knorm_b_g[l]),
            "out_norm_a_g": out_norm_a_g[l][None],
            "out_norm_b_g": out_norm_b_g[l][None],
            "w_out": w_out[l].astype(BF16),
            "norm_ffn_g": norm_ffn_g[l][None],
            "w_up": w_up[l].astype(BF16),
            "conv_w": conv_w[l],
            "conv_b": conv_b[l][None],
            "w_down": w_down[l].astype(BF16),
        }
        y_prompt = _layer(y_prompt, p)
        y_sample = _layer(y_sample, p)
    return (y_prompt, y_sample)
```

```python
import functools
import math

import jax
import jax.numpy as jnp
from jax import lax
from jax.experimental import pallas as pl
from jax.experimental.pallas import tpu as pltpu

D_MODEL = 1024
HEAD_DIM = 64
N_HEADS = 8
N_KV = 2
GROUP = N_HEADS // N_KV
WIDTH = N_HEADS * HEAD_DIM
KV_WIDTH = N_KV * HEAD_DIM
IN_WIDTH = 2 * (WIDTH + 2 * KV_WIDTH)
WINDOW = 128
GRID_W = 64
ROPE_THETA = 10000.0
D_FF = 2816
EPS = 1e-6
LOG2E = math.log2(math.e)
Q_SCALE = HEAD_DIM ** -0.5 * LOG2E

LANES = 128
NORM_CHUNK = 256
BF16_ROWS = 16
HALO = BF16_ROWS
SUM_ROWS = BF16_ROWS
AHEAD = 1
CHUNKS_PER_TRIP = 8
FF_CHUNK = 256
U_SLOTS = 1
EXP2_SAFE_RANGE = 50.0
B_TILES_PER_STEP = 8
B_SCORE_SAFE = 30.0
B_SINK_SAFE = 60.0
VMEM_LIMIT =56 * 1024 * 1024

F32 = jnp.float32
BF16 = jnp.bfloat16
F8 = jnp.float8_e4m3fn
F8_MAX = 448.0
Q8_SHIFT = 6
K8_SHIFT = 4
STACK = 4 * HEAD_DIM


def _tile(n, want):
    t = min(n, want)
    assert n % t == 0, (n, t)
    return t


def _rms(x, g):
    return x * lax.rsqrt(jnp.mean(x * x, axis=-1, keepdims=True) + EPS) * g


def _proj_kernel(x_ref, g_ref, w_ref, hg_ref, cos_ref, sin_ref,
                 qa_ref, ka_ref, va_ref, qb_ref, kb_ref, vb_ref, qa8_ref, ka8_ref):
    h = _rms(x_ref[0], g_ref[...]).astype(BF16)
    r = lax.broadcasted_iota(jnp.int32, (NORM_CHUNK, NORM_CHUNK), 0) // HEAD_DIM
    c = lax.broadcasted_iota(jnp.int32, (NORM_CHUNK, NORM_CHUNK), 1) // HEAD_DIM
    seg = (r == c).astype(BF16)

    def project(lo):
        return jnp.dot(h, w_ref[:, lo:lo + NORM_CHUNK], preferred_element_type=F32)

    def head_norm(v, lo):
        sq = v * v
        sq_hi = sq.astype(BF16)
        sq_lo = (sq - sq_hi.astype(F32)).astype(BF16)
        ss = (jnp.dot(sq_hi, seg, preferred_element_type=F32)
              + jnp.dot(sq_lo, seg, preferred_element_type=F32))
        return v * lax.rsqrt(ss * (1.0 / HEAD_DIM) + EPS) * hg_ref[:, lo:lo + NORM_CHUNK]

    lane = lax.broadcasted_iota(jnp.int32, (1, LANES), 1)
    first_half = (lane % 32) < 16
    cos = cos_ref[...]
    sin = sin_ref[...]

    def rope(v):
        partner = jnp.where(first_half, pltpu.roll(v, LANES - 16, 1), pltpu.roll(v, 16, 1))
        return v * cos + partner * sin

    def put_t(ref, first_head, v):
        vt = v.T
        ref[0, first_head] = vt[:HEAD_DIM].astype(ref.dtype)
        ref[0, first_head + 1] = vt[HEAD_DIM:].astype(ref.dtype)
        return vt

    def split8(x):
        hi = x.astype(F8)
        return hi, (x - hi.astype(F32)).astype(F8)

    def put_q8(first_head, vt):
        scaled = vt * (2.0 ** Q8_SHIFT)
        for i in range(LANES // HEAD_DIM):
            hi, lo = split8(scaled[i * HEAD_DIM:(i + 1) * HEAD_DIM])
            qa8_ref[0, first_head + i] = jnp.concatenate([hi, lo, hi, lo], axis=0)

    def put_k8(v):
        scaled = v * (2.0 ** K8_SHIFT)
        swapped = pltpu.roll(scaled, HEAD_DIM, 1)
        low = lane < HEAD_DIM
        for i, dup in enumerate((jnp.where(low, scaled, swapped), jnp.where(low, swapped, scaled))):
            hi, lo = split8(dup)
            ka8_ref[0, i] = jnp.concatenate([hi, lo], axis=1)

    def put(ref, v):
        ref[0, 0] = v[:, :HEAD_DIM].astype(ref.dtype)
        ref[0, 1] = v[:, HEAD_DIM:].astype(ref.dtype)

    def finish_q(v, lo, ref, first_head, rotary):
        q = head_norm(v, lo)
        for p in range(NORM_CHUNK // LANES):
            part = q[:, p * LANES:(p + 1) * LANES]
            hd = first_head + p * (LANES // HEAD_DIM)
            vt = put_t(ref, hd, (rope(part) if rotary else part) * Q_SCALE)
            if rotary:
                put_q8(hd, vt)

    def finish_kv(v, lo, k_ref, v_ref, rotary):
        k = head_norm(v, lo)[:, :LANES]
        k = rope(k) if rotary else k
        put(k_ref, k)
        if rotary:
            put_k8(k)
        put_t(v_ref, 0, v[:, LANES:])

    heads_per_chunk = NORM_CHUNK // HEAD_DIM
    o_qb = WIDTH + 2 * KV_WIDTH
    stages = []
    for j in range(WIDTH // NORM_CHUNK):
        stages.append((j * NORM_CHUNK, functools.partial(
            finish_q, ref=qa_ref, first_head=j * heads_per_chunk, rotary=True)))
        stages.append((o_qb + j * NORM_CHUNK, functools.partial(
            finish_q, ref=qb_ref, first_head=j * heads_per_chunk, rotary=False)))
    stages.append((WIDTH, functools.partial(finish_kv, k_ref=ka_ref, v_ref=va_ref, rotary=True)))
    stages.append((o_qb + WIDTH, functools.partial(finish_kv, k_ref=kb_ref, v_ref=vb_ref, rotary=False)))

    depth = AHEAD + 1
    pending = [project(lo) for lo, _ in stages[:depth]]
    for n, (lo, finish) in enumerate(stages):
        if n + depth < len(stages):
            pending.append(project(stages[n + depth][0]))
        finish(pending[n], lo)


def _proj(x, g, w_in, head_gain, cos, sin):
    B, S, D = x.shape
    T = _tile(S, 512)
    qt_shape = jax.ShapeDtypeStruct((B, N_HEADS, HEAD_DIM, S), BF16)
    k_shape = jax.ShapeDtypeStruct((B, N_KV, S, HEAD_DIM), BF16)
    vt_shape = jax.ShapeDtypeStruct((B, N_KV, HEAD_DIM, S), BF16)
    q8_shape = jax.ShapeDtypeStruct((B, N_HEADS, STACK, S), F8)
    k8_shape = jax.ShapeDtypeStruct((B, N_KV, S, STACK), F8)
    q8_spec = pl.BlockSpec((1, N_HEADS, STACK, T), lambda b, i: (b, 0, 0, i))
    k8_spec = pl.BlockSpec((1, N_KV, T, STACK), lambda b, i: (b, 0, i, 0))
    qt_spec = pl.BlockSpec((1, N_HEADS, HEAD_DIM, T), lambda b, i: (b, 0, 0, i))
    k_spec = pl.BlockSpec((1, N_KV, T, HEAD_DIM), lambda b, i: (b, 0, i, 0))
    vt_spec = pl.BlockSpec((1, N_KV, HEAD_DIM, T), lambda b, i: (b, 0, 0, i))
    const = lambda b, i: (0, 0)
    return pl.pallas_call(
        _proj_kernel,
        grid=(B, S // T),
        in_specs=[
            pl.BlockSpec((1, T, D), lambda b, i: (b, i, 0)),
            pl.BlockSpec((1, D), const),
            pl.BlockSpec((D, IN_WIDTH), const),
            pl.BlockSpec((1, IN_WIDTH), const),
            pl.BlockSpec((T, LANES), lambda b, i: (i, 0)),
            pl.BlockSpec((T, LANES), lambda b, i: (i, 0)),
        ],
        out_specs=[qt_spec, k_spec, vt_spec, qt_spec, k_spec, vt_spec, q8_spec, k8_spec],
        out_shape=[qt_shape, k_shape, vt_shape, qt_shape, k_shape, vt_shape, q8_shape, k8_shape],
        compiler_params=pltpu.CompilerParams(
            dimension_semantics=("parallel", "parallel"), vmem_limit_bytes=VMEM_LIMIT),
        name="proj",
    )(x, g, w_in, head_gain, cos, sin)


def _attn_a_kernel(bound_ref, qt_ref, k_ref, vt_ref, qt8_ref, k8_ref, o_ref, s_ref, *, tk, nk, cpb):
    tq = qt_ref.shape[-1]
    ones = jnp.ones((SUM_ROWS, tk), BF16)

    def scores(i, g, stacked8):
        ks = i * tk if isinstance(i, int) else pl.multiple_of(i * tk, tk)
        if stacked8:
            s = jnp.dot(k8_ref[0, 0, pl.ds(ks, tk), :], qt8_ref[0, g], preferred_element_type=F32)
            return s * (2.0 ** -(Q8_SHIFT + K8_SHIFT))
        return jnp.dot(k_ref[0, 0, pl.ds(ks, tk), :], qt_ref[0, g], preferred_element_type=F32)

    def item(w):
        return w // GROUP, w % GROUP

    def run(online, stacked8):
        for w in range(AHEAD):
            s_ref[w] = scores(*item(w), stacked8)

        def body(j, carry):
            c0 = j * cpb
            state = list(carry)
            tiles = [s_ref[w] for w in range(AHEAD)]
            for u in range(cpb * GROUP):
                dc, g_next = item(u + AHEAD)
                tiles.append(scores(jnp.minimum(c0 + dc, nk - 1), g_next, stacked8))
                dc, g = item(u)
                ks = pl.multiple_of((c0 + dc) * tk, tk)
                vt = jnp.concatenate([vt_ref[0, 0, :, pl.ds(ks, tk)], ones], axis=0)
                s = tiles[u]
                if online:
                    m, acc = state[g]
                    m_new = jnp.maximum(m, jnp.max(s, axis=0, keepdims=True))
                    p = jnp.exp2(s - m_new).astype(BF16)
                    acc = jnp.exp2(m - m_new) * acc + jnp.dot(vt, p, preferred_element_type=F32)
                    state[g] = (m_new, acc)
                else:
                    p = jnp.exp2(s)
                    l, acc = state[g]
                    state[g] = (l + jnp.sum(p, axis=0, keepdims=True),
                                acc + jnp.dot(vt[:HEAD_DIM], p.astype(BF16), preferred_element_type=F32))
            for w in range(AHEAD):
                s_ref[w] = tiles[cpb * GROUP + w]
            return tuple(state)

        acc0 = jnp.zeros((HEAD_DIM + SUM_ROWS, tq), F32)
        if online:
            init = tuple((jnp.full((1, tq), -jnp.inf, F32), acc0) for _ in range(GROUP))
            accs = [acc for (_, acc) in lax.fori_loop(0, nk // cpb, body, init)]
        else:
            init = tuple((jnp.zeros((1, tq), F32), jnp.zeros((HEAD_DIM, tq), F32)) for _ in range(GROUP))
            accs = [jnp.concatenate([acc, l], axis=0) for (l, acc) in lax.fori_loop(0, nk // cpb, body, init)]
        o = jnp.concatenate([acc[:HEAD_DIM] / acc[HEAD_DIM:HEAD_DIM + 1] for acc in accs], axis=0)
        o_ref[0] = o.T

    small = bound_ref[0] <= EXP2_SAFE_RANGE
    use8 = jnp.logical_and(small, bound_ref[1] > 0.5)
    pl.when(use8)(lambda: run(False, True))
    pl.when(jnp.logical_and(small, jnp.logical_not(use8)))(lambda: run(False, False))
    pl.when(jnp.logical_not(small))(lambda: run(True, False))


def _attn_a(score_bound, qt, k, vt, qt8, k8):
    B, _, _, S = qt.shape
    tq = _tile(S, 512)
    tk = _tile(S, 512)
    nk = S // tk
    cpb = math.gcd(nk, CHUNKS_PER_TRIP)
    return pl.pallas_call(
        functools.partial(_attn_a_kernel, tk=tk, nk=nk, cpb=cpb),
        grid=(B, N_KV, S // tq),
        in_specs=[
            pl.BlockSpec(memory_space=pltpu.SMEM),
            pl.BlockSpec((1, GROUP, HEAD_DIM, tq), lambda b, h, i: (b, h, 0, i)),
            pl.BlockSpec((1, 1, S, HEAD_DIM), lambda b, h, i: (b, h, 0, 0)),
            pl.BlockSpec((1, 1, HEAD_DIM, S), lambda b, h, i: (b, h, 0, 0)),
            pl.BlockSpec((1, GROUP, STACK, tq), lambda b, h, i: (b, h, 0, i)),
            pl.BlockSpec((1, 1, S, STACK), lambda b, h, i: (b, h, 0, 0)),
        ],
        out_specs=pl.BlockSpec((1, tq, GROUP * HEAD_DIM), lambda b, h, i: (b, i, h)),
        out_shape=jax.ShapeDtypeStruct((B, S, WIDTH), F32),
        scratch_shapes=[pltpu.VMEM((AHEAD, tk, tq), F32)],
        compiler_params=pltpu.CompilerParams(
            dimension_semantics=("parallel", "parallel", "parallel"), vmem_limit_bytes=VMEM_LIMIT),
        name="attn_a",
    )(score_bound, qt, k, vt, qt8, k8)


def _attn_b_kernel(hp_ref, qt_ref, k_ref, vt_ref, o_ref, *, seq, kw, tq):
    h = pl.program_id(1)
    n_sub = qt_ref.shape[-1] // tq
    ones = jnp.ones((SUM_ROWS, kw), BF16)

    def tile(j, small):
        t0 = (pl.program_id(2) * n_sub + j) * tq
        start = pl.multiple_of(jnp.clip(t0 - WINDOW, 0, seq - kw), LANES)
        k = k_ref[0, 0, pl.ds(start, kw), :]
        vt = vt_ref[0, 0, :, pl.ds(start, kw)]
        spos = start + lax.broadcasted_iota(jnp.int32, (kw, tq), 0)
        tpos = t0 + lax.broadcasted_iota(jnp.int32, (kw, tq), 1)
        dist = jnp.abs(tpos - spos)
        valid = dist <= WINDOW
        distf = dist.astype(F32)
        scores = [jnp.dot(k, qt_ref[0, g, :, j * tq:(j + 1) * tq], preferred_element_type=F32)
                  for g in range(GROUP)]
        outs = []
        if small:
            far = jnp.where(valid, distf, jnp.inf)
            vt_ext = jnp.concatenate([vt, ones], axis=0)
        for g in range(GROUP):
            slope = hp_ref[0, h * GROUP + g]
            sink = hp_ref[1, h * GROUP + g]
            if small:
                p = jnp.exp2(scores[g] - slope * far).astype(BF16)
                acc = jnp.dot(vt_ext, p, preferred_element_type=F32)
                denom = acc[HEAD_DIM:HEAD_DIM + 1] + jnp.exp2(jnp.full((1, tq), sink, F32))
                outs.append(acc[:HEAD_DIM] / denom)
            else:
                s = jnp.where(valid, scores[g] - slope * distf, -jnp.inf)
                m = jnp.maximum(jnp.max(s, axis=0, keepdims=True), sink)
                p = jnp.exp2(s - m)
                denom = jnp.sum(p, axis=0, keepdims=True) + jnp.exp2(sink - m)
                outs.append(jnp.dot(vt, p.astype(BF16), preferred_element_type=F32) / denom)
        o_ref[0, j * tq:(j + 1) * tq, :] = jnp.concatenate(outs, axis=0).T

    def run(small):
        for j in range(n_sub):
            tile(j, small)

    small = jnp.logical_and(hp_ref[2, 0] <= B_SCORE_SAFE, hp_ref[2, 1] <= B_SINK_SAFE)
    pl.when(small)(lambda: run(True))
    pl.when(jnp.logical_not(small))(lambda: run(False))


def _attn_b(head_params, qt, k, vt):
    B, _, _, S = qt.shape
    tq = _tile(S, 256)
    kw = min(S, tq + 2 * WINDOW)
    tqb = _tile(S, B_TILES_PER_STEP * tq)
    return pl.pallas_call(
        functools.partial(_attn_b_kernel, seq=S, kw=kw, tq=tq),
        grid=(B, N_KV, S // tqb),
        in_specs=[
            pl.BlockSpec(memory_space=pltpu.SMEM),
            pl.BlockSpec((1, GROUP, HEAD_DIM, tqb), lambda b, h, i: (b, h, 0, i)),
            pl.BlockSpec((1, 1, S, HEAD_DIM), lambda b, h, i: (b, h, 0, 0)),
            pl.BlockSpec((1, 1, HEAD_DIM, S), lambda b, h, i: (b, h, 0, 0)),
        ],
        out_specs=pl.BlockSpec((1, tqb, GROUP * HEAD_DIM), lambda b, h, i: (b, i, h)),
        out_shape=jax.ShapeDtypeStruct((B, S, WIDTH), F32),
        compiler_params=pltpu.CompilerParams(
            dimension_semantics=("parallel", "parallel", "parallel"), vmem_limit_bytes=VMEM_LIMIT),
        name="attn_b",
    )(head_params, qt, k, vt)


def _gelu_tanh(x):
    c = math.sqrt(2.0 / math.pi)
    hx = 0.5 * x
    return hx + hx * jnp.tanh(x * (c + (c * 0.044715) * (x * x)))


def _mix_ffn_kernel(ya_ref, ya_prev, ya_next, yb_ref, yb_prev, yb_next, x_ref, x_prev, x_next,
                    ga_ref, gb_ref, wout_ref, g_ref, wup_ref, cw_ref, cb_ref, wdn_ref, o_ref,
                    n_ref, h_ref, u_ref, act_ref):
    i = pl.program_id(1)
    T = x_ref.shape[1]
    tile = slice(HALO, HALO + T)

    for part, (gain, refs) in enumerate(((ga_ref, (ya_prev, ya_ref, ya_next)),
                                         (gb_ref, (yb_prev, yb_ref, yb_next)))):
        lo = 0
        for r in refs:
            rows = r.shape[1]
            n_ref[part, lo:lo + rows] = _rms(r[0], gain[...]).astype(BF16)
            lo += rows
    mixed = (jnp.dot(n_ref[0], wout_ref[:WIDTH], preferred_element_type=F32)
             + jnp.dot(n_ref[1], wout_ref[WIDTH:], preferred_element_type=F32))
    o_ref[0] = x_ref[0] + mixed[tile]

    g = g_ref[...]
    h_prev = _rms(x_prev[0] + mixed[:HALO], g) * (i > 0).astype(F32)
    h_next = _rms(x_next[0] + mixed[HALO + T:], g) * (i < pl.num_programs(1) - 1).astype(F32)
    h_ref[:HALO] = h_prev.astype(BF16)
    h_ref[tile] = _rms(o_ref[0], g).astype(BF16)
    h_ref[HALO + T:] = h_next.astype(BF16)
    h = h_ref[...]

    for j in range(D_FF // FF_CHUNK):
        slot = j % U_SLOTS
        cols = (j * FF_CHUNK, D_FF + j * FF_CHUNK)
        for part, lo in enumerate(cols):
            u_ref[slot, part] = jnp.dot(h, wup_ref[:, lo:lo + FF_CHUNK], preferred_element_type=F32)

        def conv(part, lo):
            w = cw_ref[:, lo:lo + FF_CHUNK]
            return (u_ref[slot, part, HALO - 1:HALO - 1 + T] * w[0:1]
                    + u_ref[slot, part, HALO:HALO + T] * w[1:2]
                    + u_ref[slot, part, HALO + 1:HALO + 1 + T] * w[2:3]
                    + cb_ref[:, lo:lo + FF_CHUNK])

        act = _gelu_tanh(conv(0, cols[0])) * conv(1, cols[1])
        act_ref[:, cols[0]:cols[0] + FF_CHUNK] = act.astype(BF16)
    o_ref[0] = o_ref[0] + jnp.dot(act_ref[...], wdn_ref[...], preferred_element_type=F32)


def _mix_ffn(ya, yb, x, ga, gb, w_out, g, w_up, conv_w, conv_b, w_down):
    B, S, D = x.shape
    T = _tile(S, 512)
    hb = T // HALO
    n_hb = S // HALO
    const = lambda b, i: (0, 0)
    resident = functools.partial(pl.BlockSpec, index_map=const, pipeline_mode=pl.Buffered(1))

    def rows(width):
        return [pl.BlockSpec((1, T, width), lambda b, i: (b, i, 0)),
                pl.BlockSpec((1, HALO, width), lambda b, i: (b, jnp.maximum(i * hb - 1, 0), 0)),
                pl.BlockSpec((1, HALO, width), lambda b, i: (b, jnp.minimum((i + 1) * hb, n_hb - 1), 0))]

    return pl.pallas_call(
        _mix_ffn_kernel,
        grid=(B, S // T),
        in_specs=rows(WIDTH) + rows(WIDTH) + rows(D) + [
            pl.BlockSpec((1, WIDTH), const),
            pl.BlockSpec((1, WIDTH), const),
            resident((2 * WIDTH, D)),
            pl.BlockSpec((1, D), const),
            resident((D, 2 * D_FF)),
            pl.BlockSpec((3, 2 * D_FF), const),
            pl.BlockSpec((1, 2 * D_FF), const),
            resident((D_FF, D)),
        ],
        out_specs=pl.BlockSpec((1, T, D), lambda b, i: (b, i, 0)),
        out_shape=jax.ShapeDtypeStruct((B, S, D), F32),
        scratch_shapes=[pltpu.VMEM((2, T + 2 * HALO, WIDTH), BF16),
                        pltpu.VMEM((T + 2 * HALO, D), BF16),
                        pltpu.VMEM((U_SLOTS, 2, T + 2 * HALO, FF_CHUNK), F32),
                        pltpu.VMEM((T, D_FF), BF16)],
        compiler_params=pltpu.CompilerParams(
            dimension_semantics=("parallel", "parallel"), vmem_limit_bytes=VMEM_LIMIT),
        name="mix_ffn",
    )(ya, ya, ya, yb, yb, yb, x, x, x, ga, gb, w_out, g, w_up, conv_w, conv_b, w_down)


def _rope_tables(S):
    t = jnp.arange(S, dtype=jnp.int32)
    pos = jnp.stack([(t // GRID_W).astype(F32), (t % GRID_W).astype(F32)], axis=1)
    axis_dim = HEAD_DIM // 2
    inv = ROPE_THETA ** (-jnp.arange(0, axis_dim, 2, dtype=F32) / axis_dim)
    ang = pos[:, :, None] * inv
    cos = jnp.cos(ang)
    sin = jnp.sin(ang)
    cos_h = jnp.concatenate([cos, cos], axis=-1).reshape(S, HEAD_DIM)
    sin_h = jnp.concatenate([-sin, sin], axis=-1).reshape(S, HEAD_DIM)
    reps = LANES // HEAD_DIM
    return jnp.tile(cos_h, (1, reps)), jnp.tile(sin_h, (1, reps))


def _score_bound(gq, gk):
    bound = HEAD_DIM * jnp.max(jnp.abs(gq)) * jnp.max(jnp.abs(gk)) * Q_SCALE * (1.0 + 2.0 ** -7) ** 2
    return bound.astype(F32).reshape(1)


def _attn_a_flags(gq, gk):
    root = math.sqrt(HEAD_DIM)
    q_max = root * jnp.max(jnp.abs(gq)) * (Q_SCALE * 2.0 ** Q8_SHIFT * 1.02)
    k_max = root * jnp.max(jnp.abs(gk)) * (2.0 ** K8_SHIFT * 1.02)
    fits = jnp.logical_and(q_max <= F8_MAX, k_max <= F8_MAX).astype(F32)
    return jnp.concatenate([_score_bound(gq, gk), fits.reshape(1)])


def _head_params_b(slopes, sink, gq, gk):
    sink2 = sink.astype(F32) * LOG2E
    limits = jnp.zeros((N_HEADS,), F32).at[0].set(_score_bound(gq, gk)[0]).at[1].set(jnp.max(jnp.abs(sink2)))
    return jnp.stack([slopes * LOG2E, sink2, limits])


def _layer(x, p):
    S = x.shape[1]
    cos, sin = _rope_tables(S)
    qa, ka, va, qb, kb, vb, qa8, ka8 = _proj(x, p["norm_mix_g"], p["w_in"], p["head_gain"], cos, sin)
    ya = _attn_a(p["score_bound_a"], qa, ka, va, qa8, ka8)
    yb = _attn_b(p["head_params"], qb, kb, vb)
    return _mix_ffn(ya, yb, x, p["out_norm_a_g"], p["out_norm_b_g"], p["w_out"],
                    p["norm_ffn_g"], p["w_up"], p["conv_w"], p["conv_b"], p["w_down"])


def kernel(x_prompt, x_sample, norm_mix_g, w_in, qnorm_a_g, knorm_a_g, qnorm_b_g, knorm_b_g,
           sink_b, out_norm_a_g, out_norm_b_g, w_out, norm_ffn_g, w_up, conv_w, conv_b, w_down):
    depth = w_in.shape[0]
    y_prompt, y_sample = x_prompt, x_sample
    slopes = jnp.exp2(-8.0 * jnp.arange(1, N_HEADS + 1, dtype=F32) / N_HEADS)
    ones_kv = jnp.ones((KV_WIDTH,), F32)
    for l in range(depth):
        head_gain = jnp.concatenate([
            jnp.tile(qnorm_a_g[l], N_HEADS), jnp.tile(knorm_a_g[l], N_KV), ones_kv,
            jnp.tile(qnorm_b_g[l], N_HEADS), jnp.tile(knorm_b_g[l], N_KV), ones_kv])[None]
        p = {
            "norm_mix_g": norm_mix_g[l][None],
            "w_in": w_in[l].astype(BF16),
            "head_gain": head_gain,
            "score_bound_a": _attn_a_flags(qnorm_a_g[l], knorm_a_g[l]),
            "head_params": _head_params_b(slopes, sink_b[l], qnorm_b_g[l], knorm_b_g[l]),
            "out_norm_a_g": out_norm_a_g[l][None],
            "out_norm_b_g": out_norm_b_g[l][None],
            "w_out": w_out[l].astype(BF16),
            "norm_ffn_g": norm_ffn_g[l][None],
            "w_up": w_up[l].astype(BF16),
            "conv_w": conv_w[l],
            "conv_b": conv_b[l][None],
            "w_down": w_down[l].astype(BF16),
        }
        y_prompt = _layer(y_prompt, p)
        y_sample = _layer(y_sample, p)
    return (y_prompt, y_sample)
```
